```python
import jax
import jax.numpy as jnp
from jax import lax
import numpy as np

D_MODEL = 2048
BATCH = 4
SEQ = 8192
DEPTH = 2

CTX_LEN = 256
GRID_W = 64

HEAD_DIM = 128
A_Q_HEADS = 8
A_KV_HEADS = 2
WINDOW = 128
ATT_BLOCK = 128
ROPE_THETA = 10000.0
DN_HEADS = 8
DN_HEAD_DIM = 128
DN_CHUNK = 64
DN_CONV = 5
SSD_HEADS = 32
SSD_HEAD_DIM = 64
SSD_GROUPS = 2
SSD_STATE = 128
SSD_CHUNK = 64
SSD_CONV = 5
RW_HEADS = 32
RW_HEAD_DIM = 64
RW_DECAY_LORA = 96
RW_A_LORA = 96
RW_GATE_LORA = 256
RW_LN_EPS = 64e-5
N_EXPERTS = 64
TOP_K = 8
N_GROUPS = 8
TOPK_GROUPS = 4
EXPERT_FF = 512
SHARED_FF = 512
ROUTED_SCALE = 2.5
MOE_BLOCK = 128
LN_EPS = 1e-5
RMS_EPS = 1e-6
DEEPNORM_ALPHA = (2 * DEPTH) ** 0.25
DEEPNORM_BETA = (8 * DEPTH) ** -0.25

A_Q = A_Q_HEADS * HEAD_DIM
A_KV = A_KV_HEADS * HEAD_DIM
DN_W = DN_HEADS * DN_HEAD_DIM
L0_SPLITS = (A_Q, A_KV, A_KV, DN_W, DN_W, DN_W, DN_W, DN_HEADS, DN_HEADS, DN_HEADS)
L0_IN = sum(L0_SPLITS)
L0_MIX = A_Q + DN_W
SSD_W = SSD_HEADS * SSD_HEAD_DIM
SSD_GN = SSD_GROUPS * SSD_STATE
RW_W = RW_HEADS * RW_HEAD_DIM
SSD_SPLITS = (SSD_W, SSD_W, SSD_GN, SSD_GN, SSD_HEADS, SSD_HEADS)
RW_SPLITS = (RW_W, RW_W, RW_W, RW_DECAY_LORA, RW_DECAY_LORA, RW_A_LORA, RW_GATE_LORA)
SSD_IN = sum(SSD_SPLITS)
RW_IN = sum(RW_SPLITS)
L1_IN = SSD_IN + RW_IN
L1_MIX = SSD_W + RW_W

F32 = jnp.float32

kernel_name = 'hybrid_diffusion_trunk_swa_gdn_ssd_rwkv7_moe'


def split_cols(t, sizes):
    return jnp.split(t, [int(s) for s in np.cumsum(sizes)[:-1]], axis=-1)


def to_heads(t, n_heads):
    return t.reshape(*t.shape[:-1], n_heads, t.shape[-1] // n_heads)


def layer_norm(t, g, b, eps=LN_EPS):
    tf = t.astype(F32)
    mu = jnp.mean(tf, -1, keepdims=True)
    var = jnp.mean(jnp.square(tf - mu), -1, keepdims=True)
    return ((tf - mu) * lax.rsqrt(var + eps)).astype(t.dtype) * g + b


def rms_norm(t, w):
    tf = t.astype(F32)
    return (tf * lax.rsqrt(jnp.mean(jnp.square(tf), -1, keepdims=True) + RMS_EPS)).astype(t.dtype) * w


def l2_normalize(t):
    tf = t.astype(F32)
    return tf * lax.rsqrt(jnp.sum(jnp.square(tf), -1, keepdims=True) + RMS_EPS)


def modulate(t, shift, scale):
    return t * (1.0 + scale) + shift


def centred_depthwise_conv(t, w):
    pad = w.shape[0] // 2
    return lax.conv_general_dilated(
        t, w[:, None, :].astype(t.dtype), window_strides=(1,), padding=((pad, pad),),
        dimension_numbers=('NWC', 'WIO', 'NWC'), feature_group_count=t.shape[-1])


def centred_token_shift(t):
    tp = jnp.pad(t, ((0, 0), (1, 1), (0, 0)))
    return 0.5 * (tp[:, :-2] + tp[:, 2:]) - t


def swiglu(t, w_gate, w_up, w_down):
    return (jax.nn.silu(t @ w_gate) * (t @ w_up)) @ w_down


def prefix_scan(scan_fn, ctx_in, lat_in, s0, reverse):
    flip = (lambda t: jnp.flip(t, 1)) if reverse else (lambda t: t)
    out_c, s_c = scan_fn(*[flip(t) for t in ctx_in], s0)
    out_x, _ = scan_fn(*[flip(t) for t in lat_in], s_c)
    return flip(out_c), flip(out_x)


def axial_rope(t):
    n_tok = t.shape[1]
    rows = n_tok // GRID_W
    row = jnp.repeat(jnp.arange(rows, dtype=F32), GRID_W)
    col = jnp.tile(jnp.arange(GRID_W, dtype=F32), rows)
    n_freq = HEAD_DIM // 4
    inv_freq = ROPE_THETA ** (-jnp.arange(n_freq, dtype=F32) / n_freq)
    tf = t.astype(F32)

    def rotate(u, pos):
        ang = pos[:, None] * inv_freq
        cos, sin = jnp.cos(ang)[None, :, None, :], jnp.sin(ang)[None, :, None, :]
        u1, u2 = u[..., :n_freq], u[..., n_freq:]
        return jnp.concatenate([u1 * cos - u2 * sin, u2 * cos + u1 * sin], -1)

    half = HEAD_DIM // 2
    out = jnp.concatenate([rotate(tf[..., :half], row), rotate(tf[..., half:], col)], -1)
    return out.astype(t.dtype)


def sink_softmax(s, sink):
    m = jnp.maximum(jnp.max(s, -1, keepdims=True), sink)
    p = jnp.exp(s - m)
    return p / (jnp.sum(p, -1, keepdims=True) + jnp.exp(sink - m))


def context_attention(qc, kc, vc, sink):
    bn, lc = qc.shape[:2]
    grp = A_Q_HEADS // A_KV_HEADS
    q = qc.reshape(bn, lc, A_KV_HEADS, grp, HEAD_DIM)
    s = jnp.einsum('bqhgd,bkhd->bhgqk', q, kc).astype(F32) * HEAD_DIM ** -0.5
    sink_l = sink.astype(F32).reshape(A_KV_HEADS, grp)[None, :, :, None, None]
    p = sink_softmax(s, sink_l).astype(vc.dtype)
    return jnp.einsum('bhgqk,bkhd->bqhgd', p, vc).reshape(bn, lc, A_Q)


def window_attention(q, k, v, kc, vc, sink):
    bn, s_len = q.shape[:2]
    grp = A_Q_HEADS // A_KV_HEADS
    n_blk = s_len // ATT_BLOCK
    scale = HEAD_DIM ** -0.5
    qb = q.reshape(bn, n_blk, ATT_BLOCK, A_KV_HEADS, grp, HEAD_DIM)
    pad = ((0, 0), (ATT_BLOCK, ATT_BLOCK), (0, 0), (0, 0))
    kp, vp = jnp.pad(k, pad), jnp.pad(v, pad)
    q_off = jnp.arange(ATT_BLOCK)[:, None]
    k_off = jnp.arange(3 * ATT_BLOCK)[None, :] - ATT_BLOCK
    band = jnp.abs(q_off - k_off) <= WINDOW
    sink_l = sink.astype(F32).reshape(A_KV_HEADS, grp)[None, :, :, None, None]

    def block(b):
        qblk = lax.dynamic_index_in_dim(qb, b, axis=1, keepdims=False)
        kblk = lax.dynamic_slice_in_dim(kp, b * ATT_BLOCK, 3 * ATT_BLOCK, axis=1)
        vblk = lax.dynamic_slice_in_dim(vp, b * ATT_BLOCK, 3 * ATT_BLOCK, axis=1)
        pos = b * ATT_BLOCK + k_off
        valid = band & (pos >= 0) & (pos < s_len)
        s_loc = jnp.einsum('bqhgd,bkhd->bhgqk', qblk, kblk).astype(F32) * scale
        s_loc = jnp.where(valid, s_loc, -jnp.inf)
        s_ctx = jnp.einsum('bqhgd,bkhd->bhgqk', qblk, kc).astype(F32) * scale
        p = sink_softmax(jnp.concatenate([s_loc, s_ctx], -1), sink_l).astype(v.dtype)
        return (jnp.einsum('bhgqk,bkhd->bqhgd', p[..., :3 * ATT_BLOCK], vblk)
                + jnp.einsum('bhgqk,bkhd->bqhgd', p[..., 3 * ATT_BLOCK:], vc))

    o = lax.map(block, jnp.arange(n_blk))
    return jnp.moveaxis(o, 0, 1).reshape(bn, s_len, A_Q)


def gated_delta_chunked(q, k, v, beta, g, s0):
    bn, l_len, h, dk = q.shape
    dv = v.shape[-1]
    n = l_len // DN_CHUNK
    ch = lambda t: t.astype(F32).reshape(bn, n, DN_CHUNK, *t.shape[2:])
    q, k, v, beta, g = ch(q), ch(k), ch(v), ch(beta), ch(g)
    gc = jnp.cumsum(g, axis=2)
    gct = jnp.swapaxes(gc, 2, 3)
    causal = jnp.tril(jnp.ones((DN_CHUNK, DN_CHUNK), bool))
    strict = jnp.tril(jnp.ones((DN_CHUNK, DN_CHUNK), bool), -1)
    dmask = jnp.exp(jnp.where(causal, gct[..., :, None] - gct[..., None, :], -jnp.inf))
    kb = k * beta[..., None]
    a_mat = jnp.where(strict, jnp.einsum('bnihd,bnjhd->bnhij', kb, k) * dmask, 0.0)
    eye = jnp.eye(DN_CHUNK, dtype=F32)
    rhs = jnp.swapaxes(jnp.concatenate([v * beta[..., None], kb * jnp.exp(gc)[..., None]], -1), 2, 3)
    sol = lax.linalg.triangular_solve(eye + a_mat, rhs, left_side=True, lower=True, unit_diagonal=True)
    u, w = sol[..., :dv], sol[..., dv:]
    attn = jnp.einsum('bnihd,bnjhd->bnhij', q, k) * dmask
    qg = jnp.swapaxes(q * jnp.exp(gc)[..., None], 2, 3)
    kd = jnp.swapaxes(k * jnp.exp(gc[:, :, -1:] - gc)[..., None], 2, 3)
    g_last = jnp.exp(gc[:, :, -1])

    def step(s, inp):
        u_c, w_c, attn_c, qg_c, kd_c, gl_c = inp
        v_new = u_c - jnp.einsum('bhik,bhkv->bhiv', w_c, s)
        o = jnp.einsum('bhik,bhkv->bhiv', qg_c, s) + jnp.einsum('bhij,bhjv->bhiv', attn_c, v_new)
        s = s * gl_c[..., None, None] + jnp.einsum('bhik,bhiv->bhkv', kd_c, v_new)
        return s, o

    xs = tuple(jnp.moveaxis(t, 1, 0) for t in (u, w, attn, qg, kd, g_last))
    s_fin, o = lax.scan(step, s0, xs)
    o = jnp.transpose(o, (1, 0, 3, 2, 4)).reshape(bn, l_len, h, dv)
    return o, s_fin


def gated_deltanet(cols_c, cols_x, conv_w, a_log, dt_bias, norm_w):
    def prep(cols):
        q, k, v, z, a_f, a_b, b = cols
        qkv = jax.nn.silu(centred_depthwise_conv(jnp.concatenate([q, k, v], -1), conv_w))
        q, k, v = jnp.split(qkv, 3, axis=-1)
        q = l2_normalize(to_heads(q, DN_HEADS)) * DN_HEAD_DIM ** -0.5
        k = l2_normalize(to_heads(k, DN_HEADS))
        v = to_heads(v, DN_HEADS)
        beta = jax.nn.sigmoid(b.astype(F32))
        g_f = -jnp.exp(a_log[0]) * jax.nn.softplus(a_f + dt_bias[0])
        g_b = -jnp.exp(a_log[1]) * jax.nn.softplus(a_b + dt_bias[1])
        return (q, k, v, beta, g_f), (q, k, v, beta, g_b), z

    fwd_c, bwd_c, z_c = prep(cols_c)
    fwd_x, bwd_x, z_x = prep(cols_x)
    s0 = jnp.zeros((cols_x[0].shape[0], DN_HEADS, DN_HEAD_DIM, DN_HEAD_DIM), F32)
    oc_f, ox_f = prefix_scan(gated_delta_chunked, fwd_c, fwd_x, s0, False)
    oc_b, ox_b = prefix_scan(gated_delta_chunked, bwd_c, bwd_x, s0, True)

    def finish(o, z):
        o = rms_norm(o, norm_w) * jax.nn.silu(to_heads(z, DN_HEADS).astype(F32))
        return o.reshape(*o.shape[:2], DN_W).astype(z.dtype)

    return finish(oc_f + oc_b, z_c), finish(ox_f + ox_b, z_x)


def mixer_attn_delta(hc, hx, w_in, w_out, attn_sink, conv_w, a_log, dt_bias, norm_w):
    pc = split_cols(hc @ w_in, L0_SPLITS)
    px = split_cols(hx @ w_in, L0_SPLITS)
    kc = to_heads(pc[1], A_KV_HEADS)
    vc = to_heads(pc[2], A_KV_HEADS)
    att_c = context_attention(to_heads(pc[0], A_Q_HEADS), kc, vc, attn_sink)
    att_x = window_attention(axial_rope(to_heads(px[0], A_Q_HEADS)), axial_rope(to_heads(px[1], A_KV_HEADS)),
                             to_heads(px[2], A_KV_HEADS), kc, vc, attn_sink)
    dn_c, dn_x = gated_deltanet(pc[3:], px[3:], conv_w, a_log, dt_bias, norm_w)
    oc = jnp.concatenate([att_c, dn_c], -1) @ w_out
    ox = jnp.concatenate([att_x, dn_x], -1) @ w_out
    return oc, ox


def ssd_chunked(xdt, da, bm, cm, s0):
    bn, l_len, h, p = xdt.shape
    grp, n_st = bm.shape[2:]
    rep = h // grp
    n = l_len // SSD_CHUNK
    x = xdt.astype(F32).reshape(bn, n, SSD_CHUNK, grp, rep, p)
    a = da.astype(F32).reshape(bn, n, SSD_CHUNK, grp, rep)
    bm = bm.astype(F32).reshape(bn, n, SSD_CHUNK, grp, n_st)
    cm = cm.astype(F32).reshape(bn, n, SSD_CHUNK, grp, n_st)
    acs = jnp.cumsum(a, axis=2)
    acs_t = jnp.moveaxis(acs, 2, -1)
    causal = jnp.tril(jnp.ones((SSD_CHUNK, SSD_CHUNK), bool))
    lmat = jnp.exp(jnp.where(causal, acs_t[..., :, None] - acs_t[..., None, :], -jnp.inf))
    cb = jnp.einsum('bclgn,bcsgn->bcgls', cm, bm)
    y_diag = jnp.einsum('bcgls,bcgrls,bcsgrp->bclgrp', cb, lmat, x)
    decay_in = jnp.exp(acs[:, :, -1:] - acs)
    states = jnp.einsum('bcsgn,bcsgr,bcsgrp->bcgrpn', bm, decay_in, x)
    chunk_decay = jnp.exp(acs[:, :, -1])

    def step(s, inp):
        st, dec = inp
        return s * dec[..., None, None] + st, s

    s_fin, s_prev = lax.scan(step, s0.reshape(bn, grp, rep, p, n_st),
                             (jnp.moveaxis(states, 1, 0), jnp.moveaxis(chunk_decay, 1, 0)))
    s_prev = jnp.moveaxis(s_prev, 0, 1)
    y_off = jnp.einsum('bclgn,bcgrpn,bclgr->bclgrp', cm, s_prev, jnp.exp(acs))
    y = (y_diag + y_off).reshape(bn, l_len, h, p)
    return y, s_fin.reshape(bn, h, p, n_st)


def mamba2_ssd(cols_c, cols_x, conv_w, conv_b, a_log, dt_bias, d_skip, norm_w):
    def prep(cols):
        z, xs, bm, cm, dt_f, dt_b = split_cols(cols, SSD_SPLITS)
        xbc = jax.nn.silu(centred_depthwise_conv(jnp.concatenate([xs, bm, cm], -1), conv_w) + conv_b)
        xs, bm, cm = split_cols(xbc, (SSD_W, SSD_GN, SSD_GN))
        xs = to_heads(xs, SSD_HEADS).astype(F32)
        bm = to_heads(bm, SSD_GROUPS)
        cm = to_heads(cm, SSD_GROUPS)
        dt_fwd = jax.nn.softplus(dt_f.astype(F32) + dt_bias[0])
        dt_bwd = jax.nn.softplus(dt_b.astype(F32) + dt_bias[1])
        fwd = (xs * dt_fwd[..., None], -jnp.exp(a_log[0]) * dt_fwd, bm, cm)
        bwd = (xs * dt_bwd[..., None], -jnp.exp(a_log[1]) * dt_bwd, bm, cm)
        return fwd, bwd, xs, z

    fwd_c, bwd_c, x_c, z_c = prep(cols_c)
    fwd_x, bwd_x, x_x, z_x = prep(cols_x)
    s0 = jnp.zeros((cols_x.shape[0], SSD_HEADS, SSD_HEAD_DIM, SSD_STATE), F32)
    yc_f, yx_f = prefix_scan(ssd_chunked, fwd_c, fwd_x, s0, False)
    yc_b, yx_b = prefix_scan(ssd_chunked, bwd_c, bwd_x, s0, True)

    def finish(y, xs, z):
        y = y + d_skip[:, None] * xs
        y = y.reshape(*y.shape[:2], SSD_W) * jax.nn.silu(z.astype(F32))
        y = rms_norm(to_heads(y, SSD_GROUPS), to_heads(norm_w, SSD_GROUPS))
        return y.reshape(*y.shape[:2], SSD_W).astype(z.dtype)

    return finish(yc_f + yc_b, x_c, z_c), finish(yx_f + yx_b, x_x, z_x)


def rwkv7_scan(r, w, k, v, a, b, s0):
    def step(s, inp):
        r_t, w_t, k_t, v_t, a_t, b_t = inp
        sa = jnp.einsum('bhvk,bhk->bhv', s, a_t)
        s = s * w_t[:, :, None, :] + sa[..., None] * b_t[:, :, None, :] + v_t[..., None] * k_t[:, :, None, :]
        return s, jnp.einsum('bhvk,bhk->bhv', s, r_t)

    xs = tuple(jnp.moveaxis(t.astype(F32), 1, 0) for t in (r, w, k, v, a, b))
    s_fin, y = lax.scan(step, s0, xs)
    return jnp.moveaxis(y, 0, 1), s_fin


def rwkv7_time_mix(cols_c, cols_x, mu, w0, w2, a0, a2, g2, k_k, k_a, r_k, ln_g, ln_b):
    def prep(p):
        p = p + mu * centred_token_shift(p)
        r, k, v, wl_f, wl_b, al, gl = split_cols(p, RW_SPLITS)
        a = jax.nn.sigmoid((a0 + al @ a2).astype(F32))
        g = jax.nn.sigmoid(gl) @ g2
        kk = l2_normalize(to_heads((k * k_k).astype(F32), RW_HEADS))
        k = to_heads((k * (1.0 + (a - 1.0) * k_a)).astype(F32), RW_HEADS)
        r = to_heads(r.astype(F32), RW_HEADS)
        v = to_heads(v.astype(F32), RW_HEADS)
        a = to_heads(a, RW_HEADS)

        def decay(wl, i):
            w_log = -jax.nn.softplus(-(w0[i] + jnp.tanh(wl) @ w2[i]).astype(F32)) - 0.5
            return to_heads(jnp.exp(-jnp.exp(w_log)), RW_HEADS)

        fwd = (r, decay(wl_f, 0), k, v, -kk, kk * a)
        bwd = (r, decay(wl_b, 1), k, v, -kk, kk * a)
        return fwd, bwd, (r, k, v, g)

    fwd_c, bwd_c, ext_c = prep(cols_c)
    fwd_x, bwd_x, ext_x = prep(cols_x)
    s0 = jnp.zeros((cols_x.shape[0], RW_HEADS, RW_HEAD_DIM, RW_HEAD_DIM), F32)
    yc_f, yx_f = prefix_scan(rwkv7_scan, fwd_c, fwd_x, s0, False)
    yc_b, yx_b = prefix_scan(rwkv7_scan, bwd_c, bwd_x, s0, True)

    def finish(y, ext):
        r, k, v, g = ext
        y = layer_norm(y, to_heads(ln_g, RW_HEADS), to_heads(ln_b, RW_HEADS), RW_LN_EPS)
        y = y + jnp.sum(r * k * r_k, -1, keepdims=True) * v
        return (y.reshape(*y.shape[:2], RW_W) * g).astype(g.dtype)

    return finish(yc_f + yc_b, ext_c), finish(yx_f + yx_b, ext_x)


def mixer_ssd_rwkv(hc, hx, w_in, w_out, ssd_conv_w, ssd_conv_b, ssd_a_log, ssd_dt_bias, ssd_d, ssd_norm_w,
                   rw_mu, rw_w0, rw_w2, rw_a0, rw_a2, rw_g2, rw_k_k, rw_k_a, rw_r_k, rw_ln_g, rw_ln_b):
    pc = hc @ w_in
    px = hx @ w_in
    ssd_c, ssd_x = mamba2_ssd(pc[..., :SSD_IN], px[..., :SSD_IN], ssd_conv_w, ssd_conv_b,
                              ssd_a_log, ssd_dt_bias, ssd_d, ssd_norm_w)
    rw_c, rw_x = rwkv7_time_mix(pc[..., SSD_IN:], px[..., SSD_IN:], rw_mu, rw_w0, rw_w2, rw_a0, rw_a2,
                                rw_g2, rw_k_k, rw_k_a, rw_r_k, rw_ln_g, rw_ln_b)
    oc = jnp.concatenate([ssd_c, rw_c], -1) @ w_out
    ox = jnp.concatenate([ssd_x, rw_x], -1) @ w_out
    return oc, ox


def routed_experts(h, e_idx, e_w, we_gate, we_up, we_down):
    n_tok, d = h.shape
    n_assign = n_tok * TOP_K
    n_blocks = -(-n_assign // MOE_BLOCK) + N_EXPERTS
    e_flat = e_idx.reshape(-1)
    order = jnp.argsort(e_flat)
    e_sorted = e_flat[order]
    counts = jnp.bincount(e_flat, length=N_EXPERTS)
    padded = (counts + MOE_BLOCK - 1) // MOE_BLOCK * MOE_BLOCK
    pad_end = jnp.cumsum(padded)
    pad_start = pad_end - padded
    sort_start = jnp.cumsum(counts) - counts
    dest = pad_start[e_sorted] + jnp.arange(n_assign) - sort_start[e_sorted]
    n_rows = n_blocks * MOE_BLOCK
    row_tok = jnp.full((n_rows,), n_tok, jnp.int32).at[dest].set((order // TOP_K).astype(jnp.int32))
    row_w = jnp.zeros((n_rows,), h.dtype).at[dest].set(e_w.reshape(-1)[order])
    block_expert = jnp.minimum(jnp.searchsorted(pad_end, jnp.arange(n_blocks) * MOE_BLOCK, side='right'),
                               N_EXPERTS - 1)
    h_pad = jnp.concatenate([h, jnp.zeros((1, d), h.dtype)], 0)

    def block(acc, inp):
        tok, wt, e = inp
        y = swiglu(h_pad[tok], we_gate[e], we_up[e], we_down[e]) * wt[:, None]
        return acc.at[tok].add(y), None

    acc, _ = lax.scan(block, jnp.zeros((n_tok + 1, d), h.dtype),
                      (row_tok.reshape(n_blocks, MOE_BLOCK), row_w.reshape(n_blocks, MOE_BLOCK), block_expert))
    return acc[:n_tok]


def moe(h, w_router, router_bias, we_gate, we_up, we_down, ws_gate, ws_up, ws_down):
    n_tok = h.shape[0]
    scores = jax.nn.sigmoid((h @ w_router).astype(F32))
    sel = scores + router_bias.astype(F32)
    grp_score = jnp.sum(lax.top_k(sel.reshape(n_tok, N_GROUPS, N_EXPERTS // N_GROUPS), 2)[0], -1)
    _, g_idx = lax.top_k(grp_score, TOPK_GROUPS)
    g_mask = jnp.sum(jax.nn.one_hot(g_idx, N_GROUPS, dtype=F32), 1) > 0
    sel = jnp.where(jnp.repeat(g_mask, N_EXPERTS // N_GROUPS, axis=1), sel, -jnp.inf)
    _, e_idx = lax.top_k(sel, TOP_K)
    e_w = jnp.take_along_axis(scores, e_idx, axis=1)
    e_w = e_w / jnp.sum(e_w, -1, keepdims=True) * ROUTED_SCALE
    routed = routed_experts(h, e_idx, e_w.astype(h.dtype), we_gate, we_up, we_down)
    return routed + swiglu(h, ws_gate, ws_up, ws_down)


def trunk_layer(zc, zx, c, c_ctx, mixer, w_mod, b_mod, ln1_g, ln1_b, ln2_g, ln2_b, moe_params, update_ctx):
    d = zx.shape[-1]
    sh1c, sc1c, g1c, sh2c, sc2c, g2c = jnp.split(jax.nn.silu(c_ctx) @ w_mod + b_mod, 6, axis=-1)
    sh1x, sc1x, g1x, sh2x, sc2x, g2x = jnp.split((jax.nn.silu(c) @ w_mod + b_mod)[:, None, :], 6, axis=-1)
    oc, ox = mixer(modulate(zc, sh1c, sc1c), modulate(zx, sh1x, sc1x))
    zx = layer_norm(DEEPNORM_ALPHA * zx + g1x * ox, ln1_g, ln1_b)
    hx = modulate(zx, sh2x, sc2x)
    if update_ctx:
        zc = layer_norm(DEEPNORM_ALPHA * zc + g1c * oc, ln1_g, ln1_b)
        hc = modulate(zc, sh2c, sc2c)
        n_c = hc.shape[0] * hc.shape[1]
        f = moe(jnp.concatenate([hc.reshape(-1, d), hx.reshape(-1, d)], 0), *moe_params)
        zc = layer_norm(DEEPNORM_ALPHA * zc + g2c * f[:n_c].reshape(hc.shape), ln2_g, ln2_b)
        fx = f[n_c:].reshape(hx.shape)
    else:
        fx = moe(hx.reshape(-1, d), *moe_params).reshape(hx.shape)
    zx = layer_norm(DEEPNORM_ALPHA * zx + g2x * fx, ln2_g, ln2_b)
    return zc, zx


def setup_inputs(seed: int = 0) -> dict:
    keys = iter(jax.random.split(jax.random.key(seed), 96))
    D = D_MODEL

    def normal(shape, scale):
        return jax.random.normal(next(keys), shape, F32) * scale

    def uniform(shape, lo, hi):
        return jax.random.uniform(next(keys), shape, F32, lo, hi)

    def gain(n):
        return 1.0 + normal((n,), 0.02)

    def a_log_init(n):
        return jnp.log(uniform((2, n), 1.0, 16.0))

    def dt_bias_init(n):
        dt = jnp.exp(uniform((2, n), float(np.log(1e-3)), float(np.log(1e-1))))
        return dt + jnp.log(-jnp.expm1(-dt))

    inp = {}
    inp['x'] = normal((BATCH, SEQ, D), 1.0)
    inp['c'] = normal((BATCH, D), 1.0)
    inp['ctx'] = normal((BATCH, CTX_LEN, D), 1.0)
    inp['c_ctx'] = normal((D,), 1.0)

    def add_common(p):
        inp[p + 'w_mod'] = normal((D, 6 * D), 0.5 * D ** -0.5)
        inp[p + 'b_mod'] = normal((6 * D,), 0.02)
        inp[p + 'ln1_g'] = gain(D)
        inp[p + 'ln1_b'] = normal((D,), 0.02)
        inp[p + 'ln2_g'] = gain(D)
        inp[p + 'ln2_b'] = normal((D,), 0.02)

    def add_moe(p):
        inp[p + 'w_router'] = normal((D, N_EXPERTS), D ** -0.5)
        inp[p + 'router_bias'] = normal((N_EXPERTS,), 0.01)
        inp[p + 'we_gate'] = normal((N_EXPERTS, D, EXPERT_FF), D ** -0.5)
        inp[p + 'we_up'] = normal((N_EXPERTS, D, EXPERT_FF), D ** -0.5)
        inp[p + 'we_down'] = normal((N_EXPERTS, EXPERT_FF, D), EXPERT_FF ** -0.5 * DEEPNORM_BETA)
        inp[p + 'ws_gate'] = normal((D, SHARED_FF), D ** -0.5)
        inp[p + 'ws_up'] = normal((D, SHARED_FF), D ** -0.5)
        inp[p + 'ws_down'] = normal((SHARED_FF, D), SHARED_FF ** -0.5 * DEEPNORM_BETA)

    add_common('l0_')
    inp['l0_w_in'] = normal((D, L0_IN), D ** -0.5)
    inp['l0_w_out'] = normal((L0_MIX, D), L0_MIX ** -0.5 * DEEPNORM_BETA)
    inp['l0_attn_sink'] = normal((A_Q_HEADS,), 1.0)
    inp['l0_dn_conv_w'] = normal((DN_CONV, 3 * DN_W), DN_CONV ** -0.5)
    inp['l0_dn_a_log'] = a_log_init(DN_HEADS)
    inp['l0_dn_dt_bias'] = dt_bias_init(DN_HEADS)
    inp['l0_dn_norm_w'] = gain(DN_HEAD_DIM)
    add_moe('l0_')
    add_common('l1_')
    inp['l1_w_in'] = normal((D, L1_IN), D ** -0.5)
    inp['l1_w_out'] = normal((L1_MIX, D), L1_MIX ** -0.5 * DEEPNORM_BETA)
    inp['l1_ssd_conv_w'] = normal((SSD_CONV, SSD_W + 2 * SSD_GN), SSD_CONV ** -0.5)
    inp['l1_ssd_conv_b'] = normal((SSD_W + 2 * SSD_GN,), 0.02)
    inp['l1_ssd_a_log'] = a_log_init(SSD_HEADS)
    inp['l1_ssd_dt_bias'] = dt_bias_init(SSD_HEADS)
    inp['l1_ssd_d'] = gain(SSD_HEADS)
    inp['l1_ssd_norm_w'] = gain(SSD_W)
    inp['l1_rw_mu'] = uniform((RW_IN,), 0.0, 1.0)
    inp['l1_rw_w0'] = -1.0 + normal((2, RW_W), 0.5)
    inp['l1_rw_w2'] = normal((2, RW_DECAY_LORA, RW_W), 0.5 * RW_DECAY_LORA ** -0.5)
    inp['l1_rw_a0'] = normal((RW_W,), 0.5)
    inp['l1_rw_a2'] = normal((RW_A_LORA, RW_W), 0.5 * RW_A_LORA ** -0.5)
    inp['l1_rw_g2'] = normal((RW_GATE_LORA, RW_W), RW_GATE_LORA ** -0.5)
    inp['l1_rw_k_k'] = 0.85 + normal((RW_W,), 0.05)
    inp['l1_rw_k_a'] = gain(RW_W)
    inp['l1_rw_r_k'] = normal((RW_HEADS, RW_HEAD_DIM), 0.1)
    inp['l1_rw_ln_g'] = gain(RW_W)
    inp['l1_rw_ln_b'] = normal((RW_W,), 0.02)
    add_moe('l1_')
    return inp


def reference(x, c, ctx, c_ctx,
              l0_w_mod, l0_b_mod, l0_ln1_g, l0_ln1_b, l0_ln2_g, l0_ln2_b,
              l0_w_in, l0_w_out, l0_attn_sink, l0_dn_conv_w, l0_dn_a_log, l0_dn_dt_bias, l0_dn_norm_w,
              l0_w_router, l0_router_bias, l0_we_gate, l0_we_up, l0_we_down, l0_ws_gate, l0_ws_up, l0_ws_down,
              l1_w_mod, l1_b_mod, l1_ln1_g, l1_ln1_b, l1_ln2_g, l1_ln2_b,
              l1_w_in, l1_w_out, l1_ssd_conv_w, l1_ssd_conv_b, l1_ssd_a_log, l1_ssd_dt_bias, l1_ssd_d,
              l1_ssd_norm_w, l1_rw_mu, l1_rw_w0, l1_rw_w2, l1_rw_a0, l1_rw_a2, l1_rw_g2, l1_rw_k_k, l1_rw_k_a,
              l1_rw_r_k, l1_rw_ln_g, l1_rw_ln_b,
              l1_w_router, l1_router_bias, l1_we_gate, l1_we_up, l1_we_down, l1_ws_gate, l1_ws_up, l1_ws_down):
    mixers = (
        lambda hc, hx: mixer_attn_delta(hc, hx, l0_w_in, l0_w_out, l0_attn_sink, l0_dn_conv_w,
                                        l0_dn_a_log, l0_dn_dt_bias, l0_dn_norm_w),
        lambda hc, hx: mixer_ssd_rwkv(hc, hx, l1_w_in, l1_w_out, l1_ssd_conv_w, l1_ssd_conv_b, l1_ssd_a_log,
                                      l1_ssd_dt_bias, l1_ssd_d, l1_ssd_norm_w, l1_rw_mu, l1_rw_w0, l1_rw_w2,
                                      l1_rw_a0, l1_rw_a2, l1_rw_g2, l1_rw_k_k, l1_rw_k_a, l1_rw_r_k,
                                      l1_rw_ln_g, l1_rw_ln_b),
    )
    norms = (
        (l0_w_mod, l0_b_mod, l0_ln1_g, l0_ln1_b, l0_ln2_g, l0_ln2_b),
        (l1_w_mod, l1_b_mod, l1_ln1_g, l1_ln1_b, l1_ln2_g, l1_ln2_b),
    )
    moes = (
        (l0_w_router, l0_router_bias, l0_we_gate, l0_we_up, l0_we_down, l0_ws_gate, l0_ws_up, l0_ws_down),
        (l1_w_router, l1_router_bias, l1_we_gate, l1_we_up, l1_we_down, l1_ws_gate, l1_ws_up, l1_ws_down),
    )
    zc, zx = ctx, x
    for i in range(DEPTH):
        zc, zx = trunk_layer(zc, zx, c, c_ctx, mixers[i], *norms[i], moes[i], i < DEPTH - 1)
    return zx
```

```python
import functools

import jax
import jax.numpy as jnp
import numpy as np
from jax import lax
from jax.experimental import pallas as pl
from jax.experimental.pallas import tpu as pltpu

F32 = jnp.float32
BF16 = jnp.bfloat16

DEPTH = 2
GRID_W = 64
HEAD_DIM = 128
A_Q_HEADS = 8
A_KV_HEADS = 2
WINDOW = 128
ATT_BLOCK = 128
ROPE_THETA = 10000.0
DN_HEADS = 8
DN_HEAD_DIM = 128
SSD_HEADS = 32
SSD_HEAD_DIM = 64
SSD_GROUPS = 2
SSD_STATE = 128
RW_HEADS = 32
RW_HEAD_DIM = 64
RW_DECAY_LORA = 96
RW_A_LORA = 96
RW_GATE_LORA = 256
RW_LN_EPS = 64e-5
N_EXPERTS = 64
TOP_K = 8
N_GROUPS = 8
TOPK_GROUPS = 4
ROUTED_SCALE = 2.5
LN_EPS = 1e-5
RMS_EPS = 1e-6
DEEPNORM_ALPHA = (2 * DEPTH) ** 0.25

A_Q = A_Q_HEADS * HEAD_DIM
A_KV = A_KV_HEADS * HEAD_DIM
DN_W = DN_HEADS * DN_HEAD_DIM
SSD_W = SSD_HEADS * SSD_HEAD_DIM
SSD_GN = SSD_GROUPS * SSD_STATE
RW_W = RW_HEADS * RW_HEAD_DIM

LANES = 128
SCAN_CHUNK = 64
VMEM_LIMIT = 56 * 1024 * 1024
NEG_BIG = -1e30

HIGHEST = lax.Precision.HIGHEST


def _dot(a, b, precision=None):
    return lax.dot_general(a, b, (((1,), (0,)), ((), ())), preferred_element_type=F32, precision=precision)


def _dot_nt(a, b, precision=None):
    return lax.dot_general(a, b, (((1,), (1,)), ((), ())), preferred_element_type=F32, precision=precision)


def _dot_tn(a, b, precision=None):
    return lax.dot_general(a, b, (((0,), (0,)), ((), ())), preferred_element_type=F32, precision=precision)


def _pick_tile(n, candidates):
    for c in candidates:
        if n % c == 0:
            return c
    return n


def _mm_kernel(*refs, n_pairs):
    o_ref = refs[2 * n_pairs]
    acc = None
    for a_ref, w_ref in zip(refs[:n_pairs], refs[n_pairs:2 * n_pairs]):
        p = _dot(a_ref[...].astype(BF16), w_ref[...].astype(BF16))
        acc = p if acc is None else acc + p
    o_ref[...] = acc.astype(o_ref.dtype)


def matmul(pairs, out_dtype=F32):
    m = pairs[0][0].shape[0]
    n = pairs[0][1].shape[1]
    tm = _pick_tile(m, (1024, 512, 256, 128))
    tn = _pick_tile(n, (768, 512, 384, 256, 128))
    in_specs = ([pl.BlockSpec((tm, a.shape[1]), lambda j, i: (i, 0)) for a, _ in pairs]
                + [pl.BlockSpec((w.shape[0], tn), lambda j, i: (0, j)) for _, w in pairs])
    return pl.pallas_call(
        functools.partial(_mm_kernel, n_pairs=len(pairs)),
        out_shape=jax.ShapeDtypeStruct((m, n), out_dtype),
        grid=(n // tn, m // tm),
        in_specs=in_specs,
        out_specs=pl.BlockSpec((tm, tn), lambda j, i: (i, j)),
        compiler_params=pltpu.CompilerParams(dimension_semantics=("parallel", "parallel"),
                                             vmem_limit_bytes=VMEM_LIMIT),
        name="matmul",
    )(*[a for a, _ in pairs], *[w for _, w in pairs])


def _ln_mod_kernel(z_ref, o_ref, mv_ref, g_ref, b_ref, zn_ref, h_ref, *, do_ln, do_mod):
    z = z_ref[0]
    mv = mv_ref[0, 0]
    if do_ln:
        t = DEEPNORM_ALPHA * z + mv[0:1] * o_ref[0]
        mu = jnp.mean(t, -1, keepdims=True)
        tc = t - mu
        var = jnp.mean(tc * tc, -1, keepdims=True)
        z = tc * lax.rsqrt(var + LN_EPS) * g_ref[...] + b_ref[...]
        zn_ref[0] = z
    if do_mod:
        h_ref[0] = (z * (1.0 + mv[2:3]) + mv[1:2]).astype(h_ref.dtype)


def ln_mod(z, o, mv, ln_g, ln_b, n_ctx, *, do_ln, do_mod, lat_only=False):
    bn, l_len, d = z.shape
    tr = 256
    nbc = n_ctx // tr
    off = nbc if lat_only else 0
    n_out = l_len - off * tr
    row = lambda b, i: (b, i + off, 0)
    outs, out_specs = [], []
    if do_ln:
        outs.append(jax.ShapeDtypeStruct((bn, n_out, d), F32))
        out_specs.append(pl.BlockSpec((1, tr, d), lambda b, i: (b, i, 0)))
    if do_mod:
        outs.append(jax.ShapeDtypeStruct((bn, n_out, d), BF16))
        out_specs.append(pl.BlockSpec((1, tr, d), lambda b, i: (b, i, 0)))

    def body(z_ref, o_ref, mv_ref, g_ref, b_ref, *out_refs):
        zn_ref = out_refs[0] if do_ln else None
        h_ref = out_refs[-1] if do_mod else None
        _ln_mod_kernel(z_ref, o_ref, mv_ref, g_ref, b_ref, zn_ref, h_ref, do_ln=do_ln, do_mod=do_mod)

    res = pl.pallas_call(
        body,
        out_shape=outs,
        grid=(bn, n_out // tr),
        in_specs=[pl.BlockSpec((1, tr, d), row), pl.BlockSpec((1, tr, d), row),
                  pl.BlockSpec((1, 1, 8, d), lambda b, i: (b, jnp.where(i + off < nbc, 0, 1), 0, 0)),
                  pl.BlockSpec((1, d), lambda b, i: (0, 0)), pl.BlockSpec((1, d), lambda b, i: (0, 0))],
        out_specs=out_specs,
        compiler_params=pltpu.CompilerParams(dimension_semantics=("parallel", "parallel"),
                                             vmem_limit_bytes=VMEM_LIMIT),
        name="ln_mod",
    )(z, o, mv, ln_g.reshape(1, d), ln_b.reshape(1, d))
    return res


def _rope(x, cs):
    lane = lax.broadcasted_iota(jnp.int32, x.shape, 1)
    quarter = HEAD_DIM // 4
    first = (lane % (2 * quarter)) < quarter
    partner = jnp.where(first, pltpu.roll(x, HEAD_DIM - quarter, 1), pltpu.roll(x, quarter, 1))
    return x * cs[:, :HEAD_DIM] + partner * cs[:, HEAD_DIM:]


def _attn_kernel(q_ref, kp_ref, kc_ref, kn_ref, vp_ref, vc_ref, vn_ref, kx_ref, vx_ref,
                 csp_ref, csc_ref, csn_ref, sink_ref, o_ref, *, n_ctx_blocks, s_len):
    i = pl.program_id(1)
    blk = ATT_BLOCK
    grp = A_Q_HEADS // A_KV_HEADS
    n_ctx = kx_ref.shape[1]
    csc = csc_ref[...]
    q = q_ref[0]
    q4 = jnp.concatenate([_rope(q[:, h * HEAD_DIM:(h + 1) * HEAD_DIM], csc) for h in range(grp)], 0)
    k_all = jnp.concatenate([_rope(kp_ref[0], csp_ref[...]), _rope(kc_ref[0], csc),
                             _rope(kn_ref[0], csn_ref[...]), kx_ref[0]], 0)
    v_all = jnp.concatenate([vp_ref[0], vc_ref[0], vn_ref[0], vx_ref[0]], 0)
    n_keys = 3 * blk + n_ctx
    s = _dot_nt(q4, k_all) * HEAD_DIM ** -0.5
    s = s.reshape(grp, blk, n_keys)
    tq = lax.broadcasted_iota(jnp.int32, (1, blk, n_keys), 1)
    col = lax.broadcasted_iota(jnp.int32, (1, blk, n_keys), 2)
    koff = col - blk
    pos = (i - n_ctx_blocks) * blk + koff
    local_ok = (jnp.abs(tq - koff) <= WINDOW) & (pos >= 0) & (pos < s_len) & (i >= n_ctx_blocks)
    valid = local_ok | (col >= 3 * blk)
    s = jnp.where(valid, s, -jnp.inf)
    sink = sink_ref[0]
    hsel = lax.broadcasted_iota(jnp.int32, (grp, 1, 1), 0)
    sink3 = jnp.zeros((grp, 1, 1), F32)
    for h in range(grp):
        sink3 = jnp.where(hsel == h, sink[:, h:h + 1].reshape(1, 1, 1), sink3)
    m = jnp.maximum(jnp.max(s, -1, keepdims=True), sink3)
    p = jnp.exp(s - m)
    denom = jnp.sum(p, -1, keepdims=True) + jnp.exp(sink3 - m)
    o = _dot(p.reshape(grp * blk, n_keys), v_all).reshape(grp, blk, HEAD_DIM) / denom
    for h in range(grp):
        o_ref[0, :, h * HEAD_DIM:(h + 1) * HEAD_DIM] = o[h]


def attention(p_all, cs, sink, n_ctx, q_col, k_col, v_col):
    bn, l_len, _ = p_all.shape
    blk = ATT_BLOCK
    grp = A_Q_HEADS // A_KV_HEADS
    nb = l_len // blk
    ncb = n_ctx // blk
    s_len = l_len - n_ctx
    qw = grp * HEAD_DIM
    prev = lambda i: jnp.clip(i - 1, ncb, nb - 1)
    cur = lambda i: jnp.clip(i, ncb, nb - 1)
    nxt = lambda i: jnp.clip(i + 1, ncb, nb - 1)
    kb, vb = k_col // HEAD_DIM, v_col // HEAD_DIM

    def kv_spec(colb, rowf):
        return pl.BlockSpec((1, blk, HEAD_DIM), lambda b, i, h: (b, rowf(i), colb + h))

    def cs_spec(rowf):
        return pl.BlockSpec((blk, 2 * HEAD_DIM), lambda b, i, h: (rowf(i), 0))

    return pl.pallas_call(
        functools.partial(_attn_kernel, n_ctx_blocks=ncb, s_len=s_len),
        out_shape=jax.ShapeDtypeStruct((bn, l_len, A_Q), F32),
        grid=(bn, nb, A_KV_HEADS),
        in_specs=[pl.BlockSpec((1, blk, qw), lambda b, i, h: (b, i, q_col // qw + h)),
                  kv_spec(kb, prev), kv_spec(kb, cur), kv_spec(kb, nxt),
                  kv_spec(vb, prev), kv_spec(vb, cur), kv_spec(vb, nxt),
                  pl.BlockSpec((1, n_ctx, HEAD_DIM), lambda b, i, h: (b, 0, kb + h)),
                  pl.BlockSpec((1, n_ctx, HEAD_DIM), lambda b, i, h: (b, 0, vb + h)),
                  cs_spec(prev), pl.BlockSpec((blk, 2 * HEAD_DIM), lambda b, i, h: (i, 0)), cs_spec(nxt),
                  pl.BlockSpec((1, 1, grp), lambda b, i, h: (h, 0, 0))],
        out_specs=pl.BlockSpec((1, blk, qw), lambda b, i, h: (b, i, h)),
        compiler_params=pltpu.CompilerParams(dimension_semantics=("parallel", "parallel", "parallel"),
                                             vmem_limit_bytes=VMEM_LIMIT),
        name="window_attention",
    )(p_all, p_all, p_all, p_all, p_all, p_all, p_all, p_all, p_all, cs, cs, cs,
      sink.astype(F32).reshape(A_KV_HEADS, 1, grp))


def rope_table(n_ctx, s_len):
    rows = s_len // GRID_W
    row = jnp.repeat(jnp.arange(rows, dtype=F32), GRID_W)
    colp = jnp.tile(jnp.arange(GRID_W, dtype=F32), rows)
    n_freq = HEAD_DIM // 4
    inv_freq = ROPE_THETA ** (-jnp.arange(n_freq, dtype=F32) / n_freq)
    ang_r = row[:, None] * inv_freq
    ang_c = colp[:, None] * inv_freq
    cos = jnp.concatenate([jnp.cos(ang_r), jnp.cos(ang_r), jnp.cos(ang_c), jnp.cos(ang_c)], -1)
    sin = jnp.concatenate([-jnp.sin(ang_r), jnp.sin(ang_r), -jnp.sin(ang_c), jnp.sin(ang_c)], -1)
    lat = jnp.concatenate([cos, sin], -1)
    ctx = jnp.concatenate([jnp.ones((n_ctx, HEAD_DIM), F32), jnp.zeros((n_ctx, HEAD_DIM), F32)], -1)
    return jnp.concatenate([ctx, lat], 0)


def _order_masks(c, d):
    t = lax.broadcasted_iota(jnp.int32, (c, c), 0)
    j = lax.broadcasted_iota(jnp.int32, (c, c), 1)
    rel = (t - j) * (1 - 2 * d)
    return rel >= 0, rel > 0


def _unit_inverse(m):
    c = m.shape[0]
    eye = (lax.broadcasted_iota(jnp.int32, (c, c), 0) == lax.broadcasted_iota(jnp.int32, (c, c), 1)).astype(F32)
    t = eye + m
    pw = m
    n = 1
    while 2 * n < c:
        pw = _dot(pw, pw)
        t = t + _dot(t, pw)
        n *= 2
    return t


def _scan_step(ht, r_t, a_t, v, k_h, b_h, pc, n_rk, n_rb=None, m_ab=None, m_ak=None):
    rh = _dot_nt(r_t, ht)
    if m_ab is None:
        y = rh + _dot(n_rk, v)
        return y, ht * pc + _dot_tn(v, k_h)
    u = _dot(_unit_inverse(m_ab), _dot_nt(a_t, ht) + _dot(m_ak, v))
    y = rh + _dot(n_rb, u) + _dot(n_rk, v)
    return y, ht * pc + _dot_tn(u, b_h) + _dot_tn(v, k_h)


def _time_block(d, j, ncc, nct):
    bwd = jnp.where(j < ncc, ncc - 1 - j, nct - 1 - j + ncc)
    return jnp.where(d == 0, j, bwd)


def _cum_scalars(col_g, row_g, incl):
    tri = incl.astype(F32)
    cum_col = _dot(tri, col_g, HIGHEST)
    cum_row = _dot_nt(row_g, tri, HIGHEST)
    tot = jnp.sum(col_g, 0, keepdims=True)
    return cum_col, cum_row, tot


def _dn_scan_kernel(q_ref, k_ref, v_ref, col_ref, row_ref, o_ref, st_ref, *, heads, dh):
    d = pl.program_id(1)

    @pl.when(pl.program_id(2) == 0)
    def _():
        st_ref[...] = jnp.zeros_like(st_ref)

    c = q_ref.shape[1]
    incl, strict = _order_masks(c, d)
    col = col_ref[0, 0]
    row = row_ref[0, 0, 0]
    cum_col, cum_row, tot = _cum_scalars(col[:, heads:], row[heads:], incl)
    for h in range(heads):
        sl = slice(h * dh, (h + 1) * dh)
        q, k, v = q_ref[0, :, sl], k_ref[0, :, sl], v_ref[0, :, sl]
        bc, gc = col[:, h:h + 1], col[:, heads + h:heads + h + 1]
        br, gr = row[h:h + 1], row[heads + h:heads + h + 1]
        cc, cr, tt = cum_col[:, h:h + 1], cum_row[h:h + 1], tot[:, h:h + 1]
        kkb = _dot_nt(k, k) * br
        qkb = _dot_nt(q, k) * br
        dif = cc - cr
        m_ak = kkb * jnp.exp(jnp.where(strict, dif - gc, NEG_BIG))
        m_ab = -kkb * jnp.exp(jnp.where(strict, dif - gc + gr, NEG_BIG))
        n_rk = qkb * jnp.exp(jnp.where(incl, dif, NEG_BIG))
        n_rb = -qkb * jnp.exp(jnp.where(incl, dif + gr, NEG_BIG))
        a_t = k * jnp.exp(cc - gc)
        r_t = q * jnp.exp(cc)
        k_h = k * (bc * jnp.exp(tt - cc))
        b_h = -k * (bc * jnp.exp(gc + tt - cc))
        y, ht = _scan_step(st_ref[h], r_t, a_t, v, k_h, b_h, jnp.exp(tt), n_rk, n_rb, m_ab, m_ak)
        st_ref[h] = ht
        o_ref[0, 0, :, sl] = y


def dn_scan(q, k, v, beta, g, n_ctx):
    bn, l_len, w = q.shape
    heads = beta.shape[-1]
    dh = w // heads
    c = SCAN_CHUNK
    nct, ncc = l_len // c, n_ctx // c
    col = jnp.concatenate([jnp.broadcast_to(beta[None], g.shape), g], -1)
    row = jnp.swapaxes(col.reshape(2, bn, nct, c, 2 * heads), 3, 4)
    tb = lambda d, j: _time_block(d, j, ncc, nct)
    seq = pl.BlockSpec((1, c, w), lambda b, d, j: (b, tb(d, j), 0))
    return pl.pallas_call(
        functools.partial(_dn_scan_kernel, heads=heads, dh=dh),
        out_shape=jax.ShapeDtypeStruct((2, bn, l_len, w), F32),
        grid=(bn, 2, nct),
        in_specs=[seq, seq, seq,
                  pl.BlockSpec((1, 1, c, 2 * heads), lambda b, d, j: (d, b, tb(d, j), 0)),
                  pl.BlockSpec((1, 1, 1, 2 * heads, c), lambda b, d, j: (d, b, tb(d, j), 0, 0))],
        out_specs=pl.BlockSpec((1, 1, c, w), lambda b, d, j: (d, b, tb(d, j), 0)),
        scratch_shapes=[pltpu.VMEM((heads, dh, dh), F32)],
        compiler_params=pltpu.CompilerParams(dimension_semantics=("parallel", "parallel", "arbitrary")),
        name="dn_scan",
    )(q, k, v, col, row)


def _ssd_scan_kernel(c_ref, b_ref, x_ref, col_ref, row_ref, o_ref, st_ref, *, heads, dh):
    d = pl.program_id(2)

    @pl.when(pl.program_id(3) == 0)
    def _():
        st_ref[...] = jnp.zeros_like(st_ref)

    c = c_ref.shape[1]
    incl, _ = _order_masks(c, d)
    col = col_ref[0, 0, 0]
    row = row_ref[0, 0, 0, 0]
    cum_col, cum_row, tot = _cum_scalars(col[:, heads:], row, incl)
    cm, bm = c_ref[0], b_ref[0]
    cb = _dot_nt(cm, bm)
    for h in range(heads):
        sl = slice(h * dh, (h + 1) * dh)
        cc, cr, tt = cum_col[:, h:h + 1], cum_row[h:h + 1], tot[:, h:h + 1]
        v = x_ref[0, :, sl] * col[:, h:h + 1]
        n_rk = cb * jnp.exp(jnp.where(incl, cc - cr, NEG_BIG))
        y, ht = _scan_step(st_ref[h], cm * jnp.exp(cc), None, v, bm * jnp.exp(tt - cc), None, jnp.exp(tt), n_rk)
        st_ref[h] = ht
        o_ref[0, 0, :, sl] = y


def ssd_scan(cm, bm, x, dt, da, n_ctx):
    bn, l_len, _ = x.shape
    heads = dt.shape[-1]
    grp = SSD_GROUPS
    hg = heads // grp
    dh = x.shape[-1] // heads
    n_st = cm.shape[-1] // grp
    c = SCAN_CHUNK
    nct, ncc = l_len // c, n_ctx // c
    split = lambda t: jnp.moveaxis(t.reshape(2, bn, l_len, grp, hg), 3, 2)
    col = jnp.concatenate([split(dt), split(da)], -1)
    row = jnp.swapaxes(split(da).reshape(2, bn, grp, nct, c, hg), 4, 5)
    tb = lambda d, j: _time_block(d, j, ncc, nct)
    gseq = pl.BlockSpec((1, c, n_st), lambda b, g, d, j: (b, tb(d, j), g))
    return pl.pallas_call(
        functools.partial(_ssd_scan_kernel, heads=hg, dh=dh),
        out_shape=jax.ShapeDtypeStruct((2, bn, l_len, heads * dh), F32),
        grid=(bn, grp, 2, nct),
        in_specs=[gseq, gseq,
                  pl.BlockSpec((1, c, hg * dh), lambda b, g, d, j: (b, tb(d, j), g)),
                  pl.BlockSpec((1, 1, 1, c, 2 * hg), lambda b, g, d, j: (d, b, g, tb(d, j), 0)),
                  pl.BlockSpec((1, 1, 1, 1, hg, c), lambda b, g, d, j: (d, b, g, tb(d, j), 0, 0))],
        out_specs=pl.BlockSpec((1, 1, c, hg * dh), lambda b, g, d, j: (d, b, tb(d, j), g)),
        scratch_shapes=[pltpu.VMEM((hg, dh, n_st), F32)],
        compiler_params=pltpu.CompilerParams(
            dimension_semantics=("parallel", "parallel", "parallel", "arbitrary")),
        name="ssd_scan",
    )(cm, bm, x, col, row)


def _rwkv_scan_kernel(r_ref, k_ref, v_ref, kk_ref, b_ref, lw_ref, o_ref, st_ref, *, heads, dh):
    d = pl.program_id(2)

    @pl.when(pl.program_id(3) == 0)
    def _():
        st_ref[...] = jnp.zeros_like(st_ref)

    c = r_ref.shape[1]
    incl, strict = _order_masks(c, d)
    lw_all = lw_ref[0, 0]
    cl_all = _dot(incl.astype(F32), lw_all, HIGHEST)
    tot_all = jnp.sum(lw_all, 0, keepdims=True)
    for h in range(heads):
        sl = slice(h * dh, (h + 1) * dh)
        r, k, v, kk, b = r_ref[0, :, sl], k_ref[0, :, sl], v_ref[0, :, sl], kk_ref[0, :, sl], b_ref[0, :, sl]
        lw, cl, tt = lw_all[:, sl], cl_all[:, sl], tot_all[:, sl]
        inv_p = jnp.exp(-cl)
        w_end = jnp.exp(tt - cl)
        a_t = -kk * jnp.exp(cl - lw)
        r_t = r * jnp.exp(cl)
        b_t, k_t = b * inv_p, k * inv_p
        m_ab = jnp.where(strict, _dot_nt(a_t, b_t), 0.0)
        m_ak = jnp.where(strict, _dot_nt(a_t, k_t), 0.0)
        n_rb = jnp.where(incl, _dot_nt(r_t, b_t), 0.0)
        n_rk = jnp.where(incl, _dot_nt(r_t, k_t), 0.0)
        y, ht = _scan_step(st_ref[h], r_t, a_t, v, k * w_end, b * w_end, jnp.exp(tt), n_rk, n_rb, m_ab, m_ak)
        st_ref[h] = ht
        o_ref[0, 0, :, sl] = y


def rwkv_scan(r, k, v, kk, b, lw, n_ctx, heads_per_step=8):
    bn, l_len, w = r.shape
    dh = RW_HEAD_DIM
    hb = heads_per_step
    c = SCAN_CHUNK
    nct, ncc = l_len // c, n_ctx // c
    tb = lambda d, j: _time_block(d, j, ncc, nct)
    seq = pl.BlockSpec((1, c, hb * dh), lambda b_, g, d, j: (b_, tb(d, j), g))
    return pl.pallas_call(
        functools.partial(_rwkv_scan_kernel, heads=hb, dh=dh),
        out_shape=jax.ShapeDtypeStruct((2, bn, l_len, w), F32),
        grid=(bn, w // (hb * dh), 2, nct),
        in_specs=[seq, seq, seq, seq, seq,
                  pl.BlockSpec((1, 1, c, hb * dh), lambda b_, g, d, j: (d, b_, tb(d, j), g))],
        out_specs=pl.BlockSpec((1, 1, c, hb * dh), lambda b_, g, d, j: (d, b_, tb(d, j), g)),
        scratch_shapes=[pltpu.VMEM((hb, dh, dh), F32)],
        compiler_params=pltpu.CompilerParams(
            dimension_semantics=("parallel", "parallel", "parallel", "arbitrary")),
        name="rwkv_scan",
    )(r, k, v, kk, b, lw)


def _router_kernel(h_ref, wr_ref, bias_ref, rank_ref, wt_ref, cnt_ref):
    tm = h_ref.shape[0]
    ne, ng = N_EXPERTS, N_GROUPS
    per = ne // ng
    logits = _dot_nt(wr_ref[...], h_ref[...].astype(F32), HIGHEST)
    scores = jax.nn.sigmoid(logits)
    sel = scores + bias_ref[...]
    sel3 = sel.reshape(ng, per, tm)
    idx3 = lax.broadcasted_iota(jnp.int32, (ng, per, tm), 1)
    m1 = jnp.max(sel3, 1, keepdims=True)
    first = jnp.min(jnp.where(sel3 == m1, idx3, per), 1, keepdims=True)
    m2 = jnp.max(jnp.where(idx3 == first, -jnp.inf, sel3), 1, keepdims=True)
    grp = (m1 + m2).reshape(ng, tm)
    gidx = lax.broadcasted_iota(jnp.int32, (ng, tm), 0)
    chosen = jnp.zeros((ng, tm), jnp.bool_)
    for _ in range(TOPK_GROUPS):
        gm = jnp.max(grp, 0, keepdims=True)
        gi = jnp.min(jnp.where(grp == gm, gidx, ng), 0, keepdims=True)
        hit = gidx == gi
        chosen = chosen | hit
        grp = jnp.where(hit, -jnp.inf, grp)
    selm = jnp.where(chosen.reshape(ng, 1, tm), sel3, -jnp.inf).reshape(ne, tm)
    eidx = lax.broadcasted_iota(jnp.int32, (ne, tm), 0)
    picked = jnp.zeros((ne, tm), jnp.bool_)
    for _ in range(TOP_K):
        em = jnp.max(selm, 0, keepdims=True)
        ei = jnp.min(jnp.where(selm == em, eidx, ne), 0, keepdims=True)
        hit = eidx == ei
        picked = picked | hit
        selm = jnp.where(hit, -jnp.inf, selm)
    wsel = jnp.where(picked, scores, 0.0)
    wt_ref[...] = wsel / jnp.sum(wsel, 0, keepdims=True) * ROUTED_SCALE
    onehot = picked.astype(BF16)
    before = (lax.broadcasted_iota(jnp.int32, (tm, tm), 0) < lax.broadcasted_iota(jnp.int32, (tm, tm), 1))
    rank = _dot(onehot, before.astype(BF16))
    rank_ref[...] = jnp.where(picked, rank, -1.0)
    cnt = jnp.sum(picked.astype(F32), 1, keepdims=True)
    cnt_ref[0] = jnp.broadcast_to(cnt, (ne, LANES)).astype(jnp.int32)


def _expert_kernel(cnt_ref, h_ref, rank_ref, wt_ref, wg_ref, wu_ref, wd_ref, sg_ref, su_ref, sd_ref, o_ref):
    i, e = pl.program_id(0), pl.program_id(1)
    tm = h_ref.shape[0]
    rows = 128
    h = h_ref[...]

    d = o_ref.shape[1]
    cw = 512

    @pl.when(e == 0)
    def _():
        act = (jax.nn.silu(_dot(h, sg_ref[...])) * _dot(h, su_ref[...])).astype(BF16)
        for c0 in range(0, d, cw):
            o_ref[:, c0:c0 + cw] = _dot(act, sd_ref[:, c0:c0 + cw])

    rk = rank_ref[0]
    wt = wt_ref[0]
    n_blk = (cnt_ref[i * N_EXPERTS + e] + rows - 1) // rows
    r_iota = lax.broadcasted_iota(jnp.int32, (rows, tm), 0).astype(F32)

    def body(b, carry):
        hit = rk == r_iota + (b * rows).astype(F32)
        xg = _dot(hit.astype(BF16), h).astype(BF16)
        act = jax.nn.silu(_dot(xg, wg_ref[0])) * _dot(xg, wu_ref[0])
        y = _dot(act.astype(BF16), wd_ref[0]).astype(BF16)
        gw = jnp.where(hit, wt, 0.0).astype(BF16)
        for c0 in range(0, d, cw):
            o_ref[:, c0:c0 + cw] += _dot_tn(gw, y[:, c0:c0 + cw])
        return carry

    lax.fori_loop(0, n_blk, body, 0)


def moe(h, w_router, router_bias, we_gate, we_up, we_down, ws_gate, ws_up, ws_down, tm=1024):
    n_tok, d = h.shape
    ne = N_EXPERTS
    nt = n_tok // tm
    ff = we_gate.shape[-1]
    rank, wt, cnt = pl.pallas_call(
        _router_kernel,
        out_shape=[jax.ShapeDtypeStruct((ne, n_tok), F32), jax.ShapeDtypeStruct((ne, n_tok), F32),
                   jax.ShapeDtypeStruct((nt, ne, LANES), jnp.int32)],
        grid=(nt,),
        in_specs=[pl.BlockSpec((tm, d), lambda i: (i, 0)), pl.BlockSpec((ne, d), lambda i: (0, 0)),
                  pl.BlockSpec((ne, 1), lambda i: (0, 0))],
        out_specs=[pl.BlockSpec((ne, tm), lambda i: (0, i)), pl.BlockSpec((ne, tm), lambda i: (0, i)),
                   pl.BlockSpec((1, ne, LANES), lambda i: (i, 0, 0))],
        compiler_params=pltpu.CompilerParams(dimension_semantics=("parallel",), vmem_limit_bytes=VMEM_LIMIT),
        name="moe_router",
    )(h, w_router.T.astype(F32), router_bias.astype(F32).reshape(ne, 1))
    counts = cnt[:, :, 0].reshape(-1)
    const = lambda i, e, c: (0, 0)
    once = pl.Buffered(1)
    return pl.pallas_call(
        _expert_kernel,
        out_shape=jax.ShapeDtypeStruct((n_tok, d), F32),
        grid_spec=pltpu.PrefetchScalarGridSpec(
            num_scalar_prefetch=1,
            grid=(nt, ne),
            in_specs=[pl.BlockSpec((tm, d), lambda i, e, c: (i, 0)),
                      pl.BlockSpec((1, 1, tm), lambda i, e, c: (e, 0, i)),
                      pl.BlockSpec((1, 1, tm), lambda i, e, c: (e, 0, i)),
                      pl.BlockSpec((1, d, ff), lambda i, e, c: (e, 0, 0)),
                      pl.BlockSpec((1, d, ff), lambda i, e, c: (e, 0, 0)),
                      pl.BlockSpec((1, ff, d), lambda i, e, c: (e, 0, 0)),
                      pl.BlockSpec(ws_gate.shape, const, pipeline_mode=once),
                      pl.BlockSpec(ws_up.shape, const, pipeline_mode=once),
                      pl.BlockSpec(ws_down.shape, const, pipeline_mode=once)],
            out_specs=pl.BlockSpec((tm, d), lambda i, e, c: (i, 0))),
        compiler_params=pltpu.CompilerParams(dimension_semantics=("parallel", "arbitrary"),
                                             vmem_limit_bytes=VMEM_LIMIT),
        name="moe_experts",
    )(counts, h, rank.reshape(ne, 1, n_tok), wt.reshape(ne, 1, n_tok),
      we_gate.astype(BF16), we_up.astype(BF16), we_down.astype(BF16),
      ws_gate.astype(BF16), ws_up.astype(BF16), ws_down.astype(BF16))


def _segments(t, n_ctx, fn):
    return jnp.concatenate([fn(t[:, :n_ctx]), fn(t[:, n_ctx:])], 1)


def _dw_conv(t, w):
    pad = w.shape[0] // 2
    return lax.conv_general_dilated(
        t, w[:, None, :].astype(t.dtype), window_strides=(1,), padding=((pad, pad),),
        dimension_numbers=('NWC', 'WIO', 'NWC'), feature_group_count=t.shape[-1])


def _l2n(t):
    return t * lax.rsqrt(jnp.sum(jnp.square(t), -1, keepdims=True) + RMS_EPS)


def _heads(t, n):
    return t.reshape(*t.shape[:-1], n, t.shape[-1] // n)


def _pad_cols(w, width):
    return jnp.pad(w, ((0, 0), (0, width - w.shape[1])))


def _pad_rows(w, height):
    return jnp.pad(w, ((0, height - w.shape[0]), (0, 0)))


def _mod_vectors(c, c_ctx, w_mod, b_mod):
    bn, d = c.shape
    cc = jnp.concatenate([c, c_ctx[None], jnp.zeros((8 - bn - 1, d), F32)], 0)
    mv = matmul([(jax.nn.silu(cc), w_mod.astype(BF16))]) + b_mod
    mv = mv.reshape(8, 6, d)
    lat = mv[:bn]
    ctx = jnp.broadcast_to(mv[bn][None], lat.shape)
    return jnp.stack([ctx, lat], 1)


def _pack_mv(gate, shift, scale):
    z = jnp.zeros_like(gate)
    return jnp.stack([gate, shift, scale, z, z, z, z, z], 2)


def _layer0_mixer(h, n_ctx, w_in, w_out, attn_sink, conv_w, a_log, dt_bias, norm_w, cs):
    bn, l_len, d = h.shape
    small = 3 * DN_HEADS
    w_main = A_Q + 2 * A_KV + 4 * DN_W
    w_in_p = jnp.concatenate([w_in[:, :w_main], _pad_cols(w_in[:, w_main:], LANES)], 1).astype(BF16)
    p = matmul([(h.reshape(bn * l_len, d), w_in_p)]).reshape(bn, l_len, -1)
    att = attention(p, cs, attn_sink, n_ctx, 0, A_Q, A_Q + A_KV)
    o_dn = A_Q + 2 * A_KV
    qkv = _segments(p[..., o_dn:o_dn + 3 * DN_W], n_ctx, lambda t: jax.nn.silu(_dw_conv(t, conv_w)))
    z = p[..., o_dn + 3 * DN_W:o_dn + 4 * DN_W]
    sm = p[..., w_main:w_main + small]
    a_f, a_b, bb = sm[..., :DN_HEADS], sm[..., DN_HEADS:2 * DN_HEADS], sm[..., 2 * DN_HEADS:]
    q, k, v = jnp.split(qkv, 3, -1)
    q = (_l2n(_heads(q, DN_HEADS)) * DN_HEAD_DIM ** -0.5).reshape(bn, l_len, DN_W)
    k = _l2n(_heads(k, DN_HEADS)).reshape(bn, l_len, DN_W)
    beta = jax.nn.sigmoid(bb)
    g = jnp.stack([-jnp.exp(a_log[0]) * jax.nn.softplus(a_f + dt_bias[0]),
                   -jnp.exp(a_log[1]) * jax.nn.softplus(a_b + dt_bias[1])], 0)
    o = dn_scan(q, k, v, beta, g, n_ctx)
    o = _heads(o[0] + o[1], DN_HEADS)
    o = o * lax.rsqrt(jnp.mean(jnp.square(o), -1, keepdims=True) + RMS_EPS) * norm_w
    dn = (o * jax.nn.silu(_heads(z, DN_HEADS))).reshape(bn, l_len, DN_W)
    w_out_b = w_out.astype(BF16)
    return matmul([(att.reshape(bn * l_len, A_Q), w_out_b[:A_Q]),
                   (dn.reshape(bn * l_len, DN_W), w_out_b[A_Q:])]).reshape(bn, l_len, d)


def _layer1_mixer(h, n_ctx, w_in, w_out, ssd_conv_w, ssd_conv_b, ssd_a_log, ssd_dt_bias, ssd_d, ssd_norm_w,
                  rw_mu, rw_w0, rw_w2, rw_a0, rw_a2, rw_g2, rw_k_k, rw_k_a, rw_r_k, rw_ln_g, rw_ln_b):
    bn, l_len, d = h.shape
    n_tok = bn * l_len
    p = matmul([(h.reshape(n_tok, d), _pad_cols(w_in, -(-w_in.shape[1] // 768) * 768).astype(BF16))])
    p = p.reshape(bn, l_len, -1)
    z, xs, bm, cm, dt_f, dt_b = jnp.split(
        p[..., :2 * SSD_W + 2 * SSD_GN + 2 * SSD_HEADS],
        np.cumsum([SSD_W, SSD_W, SSD_GN, SSD_GN, SSD_HEADS]).tolist(), -1)
    xbc = _segments(jnp.concatenate([xs, bm, cm], -1), n_ctx,
                    lambda t: jax.nn.silu(_dw_conv(t, ssd_conv_w) + ssd_conv_b))
    xs, bm, cm = xbc[..., :SSD_W], xbc[..., SSD_W:SSD_W + SSD_GN], xbc[..., SSD_W + SSD_GN:]
    dt = jnp.stack([jax.nn.softplus(dt_f + ssd_dt_bias[0]), jax.nn.softplus(dt_b + ssd_dt_bias[1])], 0)
    da = -jnp.exp(ssd_a_log)[:, None, None, :] * dt
    y = ssd_scan(cm, bm, xs, dt, da, n_ctx)
    y = _heads(y[0] + y[1], SSD_HEADS) + ssd_d[:, None] * _heads(xs, SSD_HEADS)
    y = y.reshape(bn, l_len, SSD_W) * jax.nn.silu(z)
    y = _heads(y, SSD_GROUPS)
    y = y * lax.rsqrt(jnp.mean(jnp.square(y), -1, keepdims=True) + RMS_EPS) * _heads(ssd_norm_w, SSD_GROUPS)
    ssd = y.reshape(bn, l_len, SSD_W)
    off = 2 * SSD_W + 2 * SSD_GN + 2 * SSD_HEADS
    rw_in = 3 * RW_W + 2 * RW_DECAY_LORA + RW_A_LORA + RW_GATE_LORA
    pr = p[..., off:off + rw_in]

    def shift(t):
        tp = jnp.pad(t, ((0, 0), (1, 1), (0, 0)))
        return 0.5 * (tp[:, :-2] + tp[:, 2:]) - t

    pr = pr + rw_mu * _segments(pr, n_ctx, shift)
    r, k, v, wl_f, wl_b, al, gl = jnp.split(
        pr, np.cumsum([RW_W, RW_W, RW_W, RW_DECAY_LORA, RW_DECAY_LORA, RW_A_LORA]).tolist(), -1)
    lora = lambda t, w: matmul([(_pad_cols(t.reshape(n_tok, -1), LANES), _pad_rows(w, LANES).astype(BF16))])
    a = jax.nn.sigmoid(rw_a0 + lora(al, rw_a2).reshape(bn, l_len, RW_W))
    g = matmul([(jax.nn.sigmoid(gl).reshape(n_tok, -1), rw_g2.astype(BF16))]).reshape(bn, l_len, RW_W)
    kk = _l2n(_heads(k * rw_k_k, RW_HEADS)).reshape(bn, l_len, RW_W)
    k = k * (1.0 + (a - 1.0) * rw_k_a)

    def log_decay(wl, i):
        w_log = -jax.nn.softplus(-(rw_w0[i] + lora(jnp.tanh(wl), rw_w2[i]).reshape(bn, l_len, RW_W))) - 0.5
        return -jnp.exp(w_log)

    lw = jnp.stack([log_decay(wl_f, 0), log_decay(wl_b, 1)], 0)
    y = rwkv_scan(r, k, v, kk, kk * a, lw, n_ctx)
    y = _heads(y[0] + y[1], RW_HEADS)
    mu = jnp.mean(y, -1, keepdims=True)
    var = jnp.mean(jnp.square(y - mu), -1, keepdims=True)
    y = (y - mu) * lax.rsqrt(var + RW_LN_EPS) * _heads(rw_ln_g, RW_HEADS) + _heads(rw_ln_b, RW_HEADS)
    rh, kh, vh = _heads(r, RW_HEADS), _heads(k, RW_HEADS), _heads(v, RW_HEADS)
    y = y + jnp.sum(rh * kh * rw_r_k, -1, keepdims=True) * vh
    rw = y.reshape(bn, l_len, RW_W) * g
    w_out_b = w_out.astype(BF16)
    return matmul([(ssd.reshape(n_tok, SSD_W), w_out_b[:SSD_W]),
                   (rw.reshape(n_tok, RW_W), w_out_b[SSD_W:])]).reshape(bn, l_len, d)


def kernel(x, c, ctx, c_ctx, l0_w_mod, l0_b_mod, l0_ln1_g, l0_ln1_b, l0_ln2_g, l0_ln2_b, l0_w_in, l0_w_out, l0_attn_sink, l0_dn_conv_w, l0_dn_a_log, l0_dn_dt_bias, l0_dn_norm_w, l0_w_router, l0_router_bias, l0_we_gate, l0_we_up, l0_we_down, l0_ws_gate, l0_ws_up, l0_ws_down, l1_w_mod, l1_b_mod, l1_ln1_g, l1_ln1_b, l1_ln2_g, l1_ln2_b, l1_w_in, l1_w_out, l1_ssd_conv_w, l1_ssd_conv_b, l1_ssd_a_log, l1_ssd_dt_bias, l1_ssd_d, l1_ssd_norm_w, l1_rw_mu, l1_rw_w0, l1_rw_w2, l1_rw_a0, l1_rw_a2, l1_rw_g2, l1_rw_k_k, l1_rw_k_a, l1_rw_r_k, l1_rw_ln_g, l1_rw_ln_b, l1_w_router, l1_router_bias, l1_we_gate, l1_we_up, l1_we_down, l1_ws_gate, l1_ws_up, l1_ws_down):
    bn, s_len, d = x.shape
    n_ctx = ctx.shape[1]
    l_len = n_ctx + s_len
    n_tok = bn * l_len
    z = jnp.concatenate([ctx, x], 1)
    cs = rope_table(n_ctx, s_len)
    mv0 = _mod_vectors(c, c_ctx, l0_w_mod, l0_b_mod)
    mv1 = _mod_vectors(c, c_ctx, l1_w_mod, l1_b_mod)
    pick = lambda mv, gi, shi, sci: _pack_mv(mv[:, :, gi], mv[:, :, shi], mv[:, :, sci])

    (h,) = ln_mod(z, z, pick(mv0, 2, 0, 1), l0_ln1_g, l0_ln1_b, n_ctx, do_ln=False, do_mod=True)
    o = _layer0_mixer(h, n_ctx, l0_w_in, l0_w_out, l0_attn_sink, l0_dn_conv_w, l0_dn_a_log, l0_dn_dt_bias,
                      l0_dn_norm_w, cs)
    z, h = ln_mod(z, o, pick(mv0, 2, 3, 4), l0_ln1_g, l0_ln1_b, n_ctx, do_ln=True, do_mod=True)
    f = moe(h.reshape(n_tok, d), l0_w_router, l0_router_bias, l0_we_gate, l0_we_up, l0_we_down,
            l0_ws_gate, l0_ws_up, l0_ws_down).reshape(bn, l_len, d)
    mv_a = _pack_mv(mv0[:, :, 5], mv1[:, :, 0], mv1[:, :, 1])
    z, h = ln_mod(z, f, mv_a, l0_ln2_g, l0_ln2_b, n_ctx, do_ln=True, do_mod=True)

    o = _layer1_mixer(h, n_ctx, l1_w_in, l1_w_out, l1_ssd_conv_w, l1_ssd_conv_b, l1_ssd_a_log, l1_ssd_dt_bias,
                      l1_ssd_d, l1_ssd_norm_w, l1_rw_mu, l1_rw_w0, l1_rw_w2, l1_rw_a0, l1_rw_a2, l1_rw_g2,
                      l1_rw_k_k, l1_rw_k_a, l1_rw_r_k, l1_rw_ln_g, l1_rw_ln_b)
    z, h = ln_mod(z, o, pick(mv1, 2, 3, 4), l1_ln1_g, l1_ln1_b, n_ctx, do_ln=True, do_mod=True)
    f = moe(h.reshape(n_tok, d), l1_w_router, l1_router_bias, l1_we_gate, l1_we_up, l1_we_down,
            l1_ws_gate, l1_ws_up, l1_ws_down).reshape(bn, l_len, d)
    (zx,) = ln_mod(z, f, pick(mv1, 5, 3, 4), l1_ln2_g, l1_ln2_b, n_ctx, do_ln=True, do_mod=False, lat_only=True)
    return zx
```

```python
import functools

import jax
import jax.numpy as jnp
import numpy as np
from jax import lax
from jax.experimental import pallas as pl
from jax.experimental.pallas import tpu as pltpu

F32 = jnp.float32
BF16 = jnp.bfloat16

DEPTH = 2
GRID_W = 64
HEAD_DIM = 128
A_Q_HEADS = 8
A_KV_HEADS = 2
WINDOW = 128
ATT_BLOCK = 128
ROPE_THETA = 10000.0
DN_HEADS = 8
DN_HEAD_DIM = 128
SSD_HEADS = 32
SSD_HEAD_DIM = 64
SSD_GROUPS = 2
SSD_STATE = 128
RW_HEADS = 32
RW_HEAD_DIM = 64
RW_DECAY_LORA = 96
RW_A_LORA = 96
RW_GATE_LORA = 256
RW_LN_EPS = 64e-5
N_EXPERTS = 64
TOP_K = 8
N_GROUPS = 8
TOPK_GROUPS = 4
ROUTED_SCALE = 2.5
LN_EPS = 1e-5
RMS_EPS = 1e-6
DEEPNORM_ALPHA = (2 * DEPTH) ** 0.25

A_Q = A_Q_HEADS * HEAD_DIM
A_KV = A_KV_HEADS * HEAD_DIM
DN_W = DN_HEADS * DN_HEAD_DIM
SSD_W = SSD_HEADS * SSD_HEAD_DIM
SSD_GN = SSD_GROUPS * SSD_STATE
RW_W = RW_HEADS * RW_HEAD_DIM

LANES = 128
SCAN_CHUNK = 64
VMEM_LIMIT = 56 * 1024 * 1024
NEG_BIG = -1e30

HIGHEST = lax.Precision.HIGHEST


def _dot_dims(a, b, dims, precision):
    if precision is None:
        a, b = a.astype(BF16), b.astype(BF16)
    return lax.dot_general(a, b, (dims, ((), ())), preferred_element_type=F32, precision=precision)


def _dot(a, b, precision=None):
    return _dot_dims(a, b, ((1,), (0,)), precision)


def _dot_nt(a, b, precision=None):
    return _dot_dims(a, b, ((1,), (1,)), precision)


def _dot_tn(a, b, precision=None):
    return _dot_dims(a, b, ((0,), (0,)), precision)


def _pick_tile(n, candidates):
    for c in candidates:
        if n % c == 0:
            return c
    return n


def _mm_kernel(*refs, n_pairs):
    o_ref = refs[2 * n_pairs]
    acc = None
    for a_ref, w_ref in zip(refs[:n_pairs], refs[n_pairs:2 * n_pairs]):
        p = _dot(a_ref[...].astype(BF16), w_ref[...].astype(BF16))
        acc = p if acc is None else acc + p
    o_ref[...] = acc.astype(o_ref.dtype)


def matmul(pairs, out_dtype=F32):
    m = pairs[0][0].shape[0]
    n = pairs[0][1].shape[1]
    tm = _pick_tile(m, (1024, 512, 256, 128))
    tn = _pick_tile(n, (768, 512, 384, 256, 128))
    in_specs = ([pl.BlockSpec((tm, a.shape[1]), lambda j, i: (i, 0)) for a, _ in pairs]
                + [pl.BlockSpec((w.shape[0], tn), lambda j, i: (0, j)) for _, w in pairs])
    return pl.pallas_call(
        functools.partial(_mm_kernel, n_pairs=len(pairs)),
        out_shape=jax.ShapeDtypeStruct((m, n), out_dtype),
        grid=(n // tn, m // tm),
        in_specs=in_specs,
        out_specs=pl.BlockSpec((tm, tn), lambda j, i: (i, j)),
        compiler_params=pltpu.CompilerParams(dimension_semantics=("parallel", "parallel"),
                                             vmem_limit_bytes=VMEM_LIMIT),
        name="matmul",
    )(*[a for a, _ in pairs], *[w for _, w in pairs])


def _ln_mod_kernel(z_ref, o_ref, mv_ref, g_ref, b_ref, zn_ref, h_ref, *, do_ln, do_mod):
    z = z_ref[0]
    mv = mv_ref[0, 0]
    if do_ln:
        t = DEEPNORM_ALPHA * z + mv[0:1] * o_ref[0]
        mu = jnp.mean(t, -1, keepdims=True)
        tc = t - mu
        var = jnp.mean(tc * tc, -1, keepdims=True)
        z = tc * lax.rsqrt(var + LN_EPS) * g_ref[...] + b_ref[...]
        zn_ref[0] = z
    if do_mod:
        h_ref[0] = (z * (1.0 + mv[2:3]) + mv[1:2]).astype(h_ref.dtype)


def ln_mod(z, o, mv, ln_g, ln_b, n_ctx, *, do_ln, do_mod, lat_only=False):
    bn, l_len, d = z.shape
    tr = 256
    nbc = n_ctx // tr
    off = nbc if lat_only else 0
    n_out = l_len - off * tr
    row = lambda b, i: (b, i + off, 0)
    outs, out_specs = [], []
    if do_ln:
        outs.append(jax.ShapeDtypeStruct((bn, n_out, d), F32))
        out_specs.append(pl.BlockSpec((1, tr, d), lambda b, i: (b, i, 0)))
    if do_mod:
        outs.append(jax.ShapeDtypeStruct((bn, n_out, d), BF16))
        out_specs.append(pl.BlockSpec((1, tr, d), lambda b, i: (b, i, 0)))

    def body(z_ref, o_ref, mv_ref, g_ref, b_ref, *out_refs):
        zn_ref = out_refs[0] if do_ln else None
        h_ref = out_refs[-1] if do_mod else None
        _ln_mod_kernel(z_ref, o_ref, mv_ref, g_ref, b_ref, zn_ref, h_ref, do_ln=do_ln, do_mod=do_mod)

    res = pl.pallas_call(
        body,
        out_shape=outs,
        grid=(bn, n_out // tr),
        in_specs=[pl.BlockSpec((1, tr, d), row), pl.BlockSpec((1, tr, d), row),
                  pl.BlockSpec((1, 1, 8, d), lambda b, i: (b, jnp.where(i + off < nbc, 0, 1), 0, 0)),
                  pl.BlockSpec((1, d), lambda b, i: (0, 0)), pl.BlockSpec((1, d), lambda b, i: (0, 0))],
        out_specs=out_specs,
        compiler_params=pltpu.CompilerParams(dimension_semantics=("parallel", "parallel"),
                                             vmem_limit_bytes=VMEM_LIMIT),
        name="ln_mod",
    )(z, o, mv, ln_g.reshape(1, d), ln_b.reshape(1, d))
    return res


def _rope(x, cs):
    lane = lax.broadcasted_iota(jnp.int32, x.shape, 1)
    quarter = HEAD_DIM // 4
    first = (lane % (2 * quarter)) < quarter
    partner = jnp.where(first, pltpu.roll(x, HEAD_DIM - quarter, 1), pltpu.roll(x, quarter, 1))
    return x * cs[:, :HEAD_DIM] + partner * cs[:, HEAD_DIM:]


def _attn_kernel(q_ref, kp_ref, kc_ref, kn_ref, vp_ref, vc_ref, vn_ref, kx_ref, vx_ref,
                 csp_ref, csc_ref, csn_ref, sink_ref, o_ref, *, n_ctx_blocks, s_len):
    i = pl.program_id(1)
    blk = ATT_BLOCK
    grp = A_Q_HEADS // A_KV_HEADS
    n_ctx = kx_ref.shape[1]
    csc = csc_ref[...]
    q = q_ref[0]
    q4 = jnp.concatenate([_rope(q[:, h * HEAD_DIM:(h + 1) * HEAD_DIM], csc) for h in range(grp)], 0)
    k_all = jnp.concatenate([_rope(kp_ref[0], csp_ref[...]), _rope(kc_ref[0], csc),
                             _rope(kn_ref[0], csn_ref[...]), kx_ref[0]], 0)
    v_all = jnp.concatenate([vp_ref[0], vc_ref[0], vn_ref[0], vx_ref[0]], 0)
    n_keys = 3 * blk + n_ctx
    s = _dot_nt(q4, k_all) * HEAD_DIM ** -0.5
    s = s.reshape(grp, blk, n_keys)
    tq = lax.broadcasted_iota(jnp.int32, (1, blk, n_keys), 1)
    col = lax.broadcasted_iota(jnp.int32, (1, blk, n_keys), 2)
    koff = col - blk
    pos = (i - n_ctx_blocks) * blk + koff
    local_ok = (jnp.abs(tq - koff) <= WINDOW) & (pos >= 0) & (pos < s_len) & (i >= n_ctx_blocks)
    valid = local_ok | (col >= 3 * blk)
    s = jnp.where(valid, s, -jnp.inf)
    sink = sink_ref[0]
    hsel = lax.broadcasted_iota(jnp.int32, (grp, 1, 1), 0)
    sink3 = jnp.zeros((grp, 1, 1), F32)
    for h in range(grp):
        sink3 = jnp.where(hsel == h, sink[:, h:h + 1].reshape(1, 1, 1), sink3)
    m = jnp.maximum(jnp.max(s, -1, keepdims=True), sink3)
    p = jnp.exp(s - m)
    denom = jnp.sum(p, -1, keepdims=True) + jnp.exp(sink3 - m)
    o = _dot(p.reshape(grp * blk, n_keys), v_all).reshape(grp, blk, HEAD_DIM) / denom
    for h in range(grp):
        o_ref[0, :, h * HEAD_DIM:(h + 1) * HEAD_DIM] = o[h]


def attention(p_all, cs, sink, n_ctx, q_col, k_col, v_col):
    bn, l_len, _ = p_all.shape
    blk = ATT_BLOCK
    grp = A_Q_HEADS // A_KV_HEADS
    nb = l_len // blk
    ncb = n_ctx // blk
    s_len = l_len - n_ctx
    qw = grp * HEAD_DIM
    prev = lambda i: jnp.clip(i - 1, ncb, nb - 1)
    cur = lambda i: jnp.clip(i, ncb, nb - 1)
    nxt = lambda i: jnp.clip(i + 1, ncb, nb - 1)
    kb, vb = k_col // HEAD_DIM, v_col // HEAD_DIM

    def kv_spec(colb, rowf):
        return pl.BlockSpec((1, blk, HEAD_DIM), lambda b, i, h: (b, rowf(i), colb + h))

    def cs_spec(rowf):
        return pl.BlockSpec((blk, 2 * HEAD_DIM), lambda b, i, h: (rowf(i), 0))

    return pl.pallas_call(
        functools.partial(_attn_kernel, n_ctx_blocks=ncb, s_len=s_len),
        out_shape=jax.ShapeDtypeStruct((bn, l_len, A_Q), F32),
        grid=(bn, nb, A_KV_HEADS),
        in_specs=[pl.BlockSpec((1, blk, qw), lambda b, i, h: (b, i, q_col // qw + h)),
                  kv_spec(kb, prev), kv_spec(kb, cur), kv_spec(kb, nxt),
                  kv_spec(vb, prev), kv_spec(vb, cur), kv_spec(vb, nxt),
                  pl.BlockSpec((1, n_ctx, HEAD_DIM), lambda b, i, h: (b, 0, kb + h)),
                  pl.BlockSpec((1, n_ctx, HEAD_DIM), lambda b, i, h: (b, 0, vb + h)),
                  cs_spec(prev), pl.BlockSpec((blk, 2 * HEAD_DIM), lambda b, i, h: (i, 0)), cs_spec(nxt),
                  pl.BlockSpec((1, 1, grp), lambda b, i, h: (h, 0, 0))],
        out_specs=pl.BlockSpec((1, blk, qw), lambda b, i, h: (b, i, h)),
        compiler_params=pltpu.CompilerParams(dimension_semantics=("parallel", "parallel", "parallel"),
                                             vmem_limit_bytes=VMEM_LIMIT),
        name="window_attention",
    )(p_all, p_all, p_all, p_all, p_all, p_all, p_all, p_all, p_all, cs, cs, cs,
      sink.astype(F32).reshape(A_KV_HEADS, 1, grp))


def rope_table(n_ctx, s_len):
    rows = s_len // GRID_W
    row = jnp.repeat(jnp.arange(rows, dtype=F32), GRID_W)
    colp = jnp.tile(jnp.arange(GRID_W, dtype=F32), rows)
    n_freq = HEAD_DIM // 4
    inv_freq = ROPE_THETA ** (-jnp.arange(n_freq, dtype=F32) / n_freq)
    ang_r = row[:, None] * inv_freq
    ang_c = colp[:, None] * inv_freq
    cos = jnp.concatenate([jnp.cos(ang_r), jnp.cos(ang_r), jnp.cos(ang_c), jnp.cos(ang_c)], -1)
    sin = jnp.concatenate([-jnp.sin(ang_r), jnp.sin(ang_r), -jnp.sin(ang_c), jnp.sin(ang_c)], -1)
    lat = jnp.concatenate([cos, sin], -1)
    ctx = jnp.concatenate([jnp.ones((n_ctx, HEAD_DIM), F32), jnp.zeros((n_ctx, HEAD_DIM), F32)], -1)
    return jnp.concatenate([ctx, lat], 0)


def _order_masks(c, d):
    t = lax.broadcasted_iota(jnp.int32, (c, c), 0)
    j = lax.broadcasted_iota(jnp.int32, (c, c), 1)
    rel = (t - j) * (1 - 2 * d)
    return rel >= 0, rel > 0


def _unit_inverses(ms):
    c = ms[0].shape[0]
    eye = (lax.broadcasted_iota(jnp.int32, (c, c), 0) == lax.broadcasted_iota(jnp.int32, (c, c), 1)).astype(F32)
    ts = [eye + m for m in ms]
    pws = ms
    n = 1
    while 2 * n < c:
        pws = [_dot(p, p) for p in pws]
        ts = [t + _dot(t, p) for t, p in zip(ts, pws)]
        n *= 2
    return ts


def _scan_chunk(hts, r_t, a_t, v, k_h, b_h, pc, n_rk, n_rb=None, m_ab=None, m_ak=None):
    zipm = lambda f, *ls: [f(*xs) for xs in zip(*ls)]
    if m_ab is None:
        y0 = zipm(_dot, n_rk, v)
        s0 = zipm(_dot_tn, v, k_h)
        ys = zipm(lambda r, ht, y: _dot_nt(r, ht) + y, r_t, hts, y0)
        return ys, zipm(lambda ht, p, s: ht * p + s, hts, pc, s0)
    ts = _unit_inverses(m_ab)
    mv = zipm(_dot, m_ak, v)
    w = zipm(_dot, ts, a_t)
    u0 = zipm(_dot, ts, mv)
    r_hat = zipm(lambda r, n, w_: r + _dot(n, w_), r_t, n_rb, w)
    y0 = zipm(lambda nb, u, nk, v_: _dot(nb, u) + _dot(nk, v_), n_rb, u0, n_rk, v)
    g = zipm(_dot_tn, w, b_h)
    s0 = zipm(lambda u, b, v_, k: _dot_tn(u, b) + _dot_tn(v_, k), u0, b_h, v, k_h)
    ys = zipm(lambda r, ht, y: _dot_nt(r, ht) + y, r_hat, hts, y0)
    return ys, zipm(lambda ht, p, g_, s: ht * p + _dot(ht, g_) + s, hts, pc, g, s0)


def _time_block(d, j, ncc, nct):
    bwd = jnp.where(j < ncc, ncc - 1 - j, nct - 1 - j + ncc)
    return jnp.where(d == 0, j, bwd)


def _cum_scalars(col_g, row_g, incl):
    tri = incl.astype(F32)
    cum_col = _dot(tri, col_g, HIGHEST)
    cum_row = _dot_nt(row_g, tri, HIGHEST)
    tot = jnp.sum(col_g, 0, keepdims=True)
    return cum_col, cum_row, tot


def _dn_scan_kernel(q_ref, k_ref, v_ref, col_ref, row_ref, o_ref, st_ref, *, heads, dh):
    d = pl.program_id(1)

    @pl.when(pl.program_id(2) == 0)
    def _():
        st_ref[...] = jnp.zeros_like(st_ref)

    c = q_ref.shape[1]
    incl, strict = _order_masks(c, d)
    col = col_ref[0, 0]
    row = row_ref[0, 0, 0]
    cum_col, cum_row, tot = _cum_scalars(col[:, heads:], row[heads:], incl)
    hs = range(heads)
    sls = [slice(h * dh, (h + 1) * dh) for h in hs]
    qs, ks, vs = ([ref[0, :, sl] for sl in sls] for ref in (q_ref, k_ref, v_ref))
    kks = [_dot_nt(k, k) for k in ks]
    qks = [_dot_nt(q, k) for q, k in zip(qs, ks)]
    args = [[] for _ in range(10)]
    for h in hs:
        q, k = qs[h], ks[h]
        bc, gc = col[:, h:h + 1], col[:, heads + h:heads + h + 1]
        br, gr = row[h:h + 1], row[heads + h:heads + h + 1]
        cc, cr, tt = cum_col[:, h:h + 1], cum_row[h:h + 1], tot[:, h:h + 1]
        kkb = kks[h] * br
        qkb = qks[h] * br
        dif = cc - cr
        m_ak = kkb * jnp.exp(jnp.where(strict, dif - gc, NEG_BIG))
        m_ab = -kkb * jnp.exp(jnp.where(strict, dif - gc + gr, NEG_BIG))
        n_rk = qkb * jnp.exp(jnp.where(incl, dif, NEG_BIG))
        n_rb = -qkb * jnp.exp(jnp.where(incl, dif + gr, NEG_BIG))
        a_t = k * jnp.exp(cc - gc)
        r_t = q * jnp.exp(cc)
        k_h = k * (bc * jnp.exp(tt - cc))
        b_h = -k * (bc * jnp.exp(gc + tt - cc))
        for lst, val in zip(args, (r_t, a_t, vs[h], k_h, b_h, jnp.exp(tt), n_rk, n_rb, m_ab, m_ak)):
            lst.append(val)
    ys, hts = _scan_chunk([st_ref[h] for h in hs], *args)
    for h in hs:
        st_ref[h] = hts[h]
        o_ref[0, 0, :, sls[h]] = ys[h]


def dn_scan(q, k, v, beta, g, n_ctx):
    bn, l_len, w = q.shape
    heads = beta.shape[-1]
    dh = w // heads
    c = SCAN_CHUNK
    nct, ncc = l_len // c, n_ctx // c
    col = jnp.concatenate([jnp.broadcast_to(beta[None], g.shape), g], -1)
    row = jnp.swapaxes(col.reshape(2, bn, nct, c, 2 * heads), 3, 4)
    tb = lambda d, j: _time_block(d, j, ncc, nct)
    seq = pl.BlockSpec((1, c, w), lambda b, d, j: (b, tb(d, j), 0))
    return pl.pallas_call(
        functools.partial(_dn_scan_kernel, heads=heads, dh=dh),
        out_shape=jax.ShapeDtypeStruct((2, bn, l_len, w), F32),
        grid=(bn, 2, nct),
        in_specs=[seq, seq, seq,
                  pl.BlockSpec((1, 1, c, 2 * heads), lambda b, d, j: (d, b, tb(d, j), 0)),
                  pl.BlockSpec((1, 1, 1, 2 * heads, c), lambda b, d, j: (d, b, tb(d, j), 0, 0))],
        out_specs=pl.BlockSpec((1, 1, c, w), lambda b, d, j: (d, b, tb(d, j), 0)),
        scratch_shapes=[pltpu.VMEM((heads, dh, dh), F32)],
        compiler_params=pltpu.CompilerParams(dimension_semantics=("parallel", "parallel", "arbitrary")),
        name="dn_scan",
    )(q, k, v, col, row)


def _ssd_scan_kernel(c_ref, b_ref, x_ref, col_ref, row_ref, o_ref, st_ref, *, heads, dh):
    d = pl.program_id(2)

    @pl.when(pl.program_id(3) == 0)
    def _():
        st_ref[...] = jnp.zeros_like(st_ref)

    c = c_ref.shape[1]
    incl, _ = _order_masks(c, d)
    col = col_ref[0, 0, 0]
    row = row_ref[0, 0, 0, 0]
    cum_col, cum_row, tot = _cum_scalars(col[:, heads:], row, incl)
    cm, bm = c_ref[0], b_ref[0]
    cb = _dot_nt(cm, bm)
    hs = range(heads)
    sls = [slice(h * dh, (h + 1) * dh) for h in hs]
    args = [[] for _ in range(7)]
    for h in hs:
        cc, cr, tt = cum_col[:, h:h + 1], cum_row[h:h + 1], tot[:, h:h + 1]
        v = x_ref[0, :, sls[h]] * col[:, h:h + 1]
        n_rk = cb * jnp.exp(jnp.where(incl, cc - cr, NEG_BIG))
        for lst, val in zip(args, (cm * jnp.exp(cc), None, v, bm * jnp.exp(tt - cc), None, jnp.exp(tt), n_rk)):
            lst.append(val)
    ys, hts = _scan_chunk([st_ref[h] for h in hs], *args)
    for h in hs:
        st_ref[h] = hts[h]
        o_ref[0, 0, :, sls[h]] = ys[h]


def ssd_scan(cm, bm, x, dt, da, n_ctx):
    bn, l_len, _ = x.shape
    heads = dt.shape[-1]
    grp = SSD_GROUPS
    hg = heads // grp
    dh = x.shape[-1] // heads
    n_st = cm.shape[-1] // grp
    c = SCAN_CHUNK
    nct, ncc = l_len // c, n_ctx // c
    split = lambda t: jnp.moveaxis(t.reshape(2, bn, l_len, grp, hg), 3, 2)
    col = jnp.concatenate([split(dt), split(da)], -1)
    row = jnp.swapaxes(split(da).reshape(2, bn, grp, nct, c, hg), 4, 5)
    tb = lambda d, j: _time_block(d, j, ncc, nct)
    gseq = pl.BlockSpec((1, c, n_st), lambda b, g, d, j: (b, tb(d, j), g))
    return pl.pallas_call(
        functools.partial(_ssd_scan_kernel, heads=hg, dh=dh),
        out_shape=jax.ShapeDtypeStruct((2, bn, l_len, heads * dh), F32),
        grid=(bn, grp, 2, nct),
        in_specs=[gseq, gseq,
                  pl.BlockSpec((1, c, hg * dh), lambda b, g, d, j: (b, tb(d, j), g)),
                  pl.BlockSpec((1, 1, 1, c, 2 * hg), lambda b, g, d, j: (d, b, g, tb(d, j), 0)),
                  pl.BlockSpec((1, 1, 1, 1, hg, c), lambda b, g, d, j: (d, b, g, tb(d, j), 0, 0))],
        out_specs=pl.BlockSpec((1, 1, c, hg * dh), lambda b, g, d, j: (d, b, tb(d, j), g)),
        scratch_shapes=[pltpu.VMEM((hg, dh, n_st), F32)],
        compiler_params=pltpu.CompilerParams(
            dimension_semantics=("parallel", "parallel", "parallel", "arbitrary")),
        name="ssd_scan",
    )(cm, bm, x, col, row)


def _rwkv_scan_kernel(r_ref, k_ref, v_ref, kk_ref, b_ref, lw_ref, o_ref, st_ref, *, heads, dh):
    d = pl.program_id(2)

    @pl.when(pl.program_id(3) == 0)
    def _():
        st_ref[...] = jnp.zeros_like(st_ref)

    c = r_ref.shape[1]
    incl, strict = _order_masks(c, d)
    lw_all = lw_ref[0, 0]
    cl_all = _dot(incl.astype(F32), lw_all, HIGHEST)
    tot_all = jnp.sum(lw_all, 0, keepdims=True)
    hs = range(heads)
    sls = [slice(h * dh, (h + 1) * dh) for h in hs]
    args = [[] for _ in range(6)]
    grams = []
    for h in hs:
        sl = sls[h]
        r, k, v, kk, b = r_ref[0, :, sl], k_ref[0, :, sl], v_ref[0, :, sl], kk_ref[0, :, sl], b_ref[0, :, sl]
        lw, cl, tt = lw_all[:, sl], cl_all[:, sl], tot_all[:, sl]
        inv_p = jnp.exp(-cl)
        w_end = jnp.exp(tt - cl)
        a_t = -kk * jnp.exp(cl - lw)
        r_t = r * jnp.exp(cl)
        grams.append((a_t, r_t, b * inv_p, k * inv_p))
        for lst, val in zip(args, (r_t, a_t, v, k * w_end, b * w_end, jnp.exp(tt))):
            lst.append(val)
    m_ab = [jnp.where(strict, _dot_nt(a, b), 0.0) for a, _, b, _ in grams]
    m_ak = [jnp.where(strict, _dot_nt(a, k), 0.0) for a, _, _, k in grams]
    n_rb = [jnp.where(incl, _dot_nt(r, b), 0.0) for _, r, b, _ in grams]
    n_rk = [jnp.where(incl, _dot_nt(r, k), 0.0) for _, r, _, k in grams]
    ys, hts = _scan_chunk([st_ref[h] for h in hs], *args, n_rk, n_rb, m_ab, m_ak)
    for h in hs:
        st_ref[h] = hts[h]
        o_ref[0, 0, :, sls[h]] = ys[h]


def rwkv_scan(r, k, v, kk, b, lw, n_ctx, heads_per_step=8):
    bn, l_len, w = r.shape
    dh = RW_HEAD_DIM
    hb = heads_per_step
    c = SCAN_CHUNK
    nct, ncc = l_len // c, n_ctx // c
    tb = lambda d, j: _time_block(d, j, ncc, nct)
    seq = pl.BlockSpec((1, c, hb * dh), lambda b_, g, d, j: (b_, tb(d, j), g))
    return pl.pallas_call(
        functools.partial(_rwkv_scan_kernel, heads=hb, dh=dh),
        out_shape=jax.ShapeDtypeStruct((2, bn, l_len, w), F32),
        grid=(bn, w // (hb * dh), 2, nct),
        in_specs=[seq, seq, seq, seq, seq,
                  pl.BlockSpec((1, 1, c, hb * dh), lambda b_, g, d, j: (d, b_, tb(d, j), g))],
        out_specs=pl.BlockSpec((1, 1, c, hb * dh), lambda b_, g, d, j: (d, b_, tb(d, j), g)),
        scratch_shapes=[pltpu.VMEM((hb, dh, dh), F32)],
        compiler_params=pltpu.CompilerParams(
            dimension_semantics=("parallel", "parallel", "parallel", "arbitrary")),
        name="rwkv_scan",
    )(r, k, v, kk, b, lw)


def _router_kernel(h_ref, wr_ref, bias_ref, rank_ref, wt_ref, cnt_ref):
    tm = h_ref.shape[0]
    ne, ng = N_EXPERTS, N_GROUPS
    per = ne // ng
    logits = _dot_nt(wr_ref[...], h_ref[...].astype(F32), HIGHEST)
    scores = jax.nn.sigmoid(logits)
    sel = scores + bias_ref[...]
    sel3 = sel.reshape(ng, per, tm)
    idx3 = lax.broadcasted_iota(jnp.int32, (ng, per, tm), 1)
    m1 = jnp.max(sel3, 1, keepdims=True)
    first = jnp.min(jnp.where(sel3 == m1, idx3, per), 1, keepdims=True)
    m2 = jnp.max(jnp.where(idx3 == first, -jnp.inf, sel3), 1, keepdims=True)
    grp = (m1 + m2).reshape(ng, tm)
    gidx = lax.broadcasted_iota(jnp.int32, (ng, tm), 0)
    chosen = jnp.zeros((ng, tm), jnp.bool_)
    for _ in range(TOPK_GROUPS):
        gm = jnp.max(grp, 0, keepdims=True)
        gi = jnp.min(jnp.where(grp == gm, gidx, ng), 0, keepdims=True)
        hit = gidx == gi
        chosen = chosen | hit
        grp = jnp.where(hit, -jnp.inf, grp)
    selm = jnp.where(chosen.reshape(ng, 1, tm), sel3, -jnp.inf).reshape(ne, tm)
    eidx = lax.broadcasted_iota(jnp.int32, (ne, tm), 0)
    picked = jnp.zeros((ne, tm), jnp.bool_)
    for _ in range(TOP_K):
        em = jnp.max(selm, 0, keepdims=True)
        ei = jnp.min(jnp.where(selm == em, eidx, ne), 0, keepdims=True)
        hit = eidx == ei
        picked = picked | hit
        selm = jnp.where(hit, -jnp.inf, selm)
    wsel = jnp.where(picked, scores, 0.0)
    wt_ref[...] = wsel / jnp.sum(wsel, 0, keepdims=True) * ROUTED_SCALE
    onehot = picked.astype(BF16)
    before = (lax.broadcasted_iota(jnp.int32, (tm, tm), 0) < lax.broadcasted_iota(jnp.int32, (tm, tm), 1))
    rank = _dot(onehot, before.astype(BF16))
    rank_ref[...] = jnp.where(picked, rank, -1.0)
    cnt = jnp.sum(picked.astype(F32), 1, keepdims=True)
    cnt_ref[0] = jnp.broadcast_to(cnt, (ne, LANES)).astype(jnp.int32)


def _expert_kernel(cnt_ref, h_ref, rank_ref, wt_ref, wg_ref, wu_ref, wd_ref, sg_ref, su_ref, sd_ref, o_ref):
    i, e = pl.program_id(0), pl.program_id(1)
    tm = h_ref.shape[0]
    rows = 128
    h = h_ref[...]

    d = o_ref.shape[1]
    cw = 512

    @pl.when(e == 0)
    def _():
        act = (jax.nn.silu(_dot(h, sg_ref[...])) * _dot(h, su_ref[...])).astype(BF16)
        for c0 in range(0, d, cw):
            o_ref[:, c0:c0 + cw] = _dot(act, sd_ref[:, c0:c0 + cw])

    rk = rank_ref[0]
    wt = wt_ref[0]
    n_blk = (cnt_ref[i * N_EXPERTS + e] + rows - 1) // rows
    r_iota = lax.broadcasted_iota(jnp.int32, (rows, tm), 0).astype(F32)

    def body(b, carry):
        hit = rk == r_iota + (b * rows).astype(F32)
        xg = _dot(hit.astype(BF16), h).astype(BF16)
        act = jax.nn.silu(_dot(xg, wg_ref[0])) * _dot(xg, wu_ref[0])
        y = _dot(act.astype(BF16), wd_ref[0]).astype(BF16)
        gw = jnp.where(hit, wt, 0.0).astype(BF16)
        for c0 in range(0, d, cw):
            o_ref[:, c0:c0 + cw] += _dot_tn(gw, y[:, c0:c0 + cw])
        return carry

    lax.fori_loop(0, n_blk, body, 0)


def moe(h, w_router, router_bias, we_gate, we_up, we_down, ws_gate, ws_up, ws_down, tm=1024):
    n_tok, d = h.shape
    ne = N_EXPERTS
    nt = n_tok // tm
    ff = we_gate.shape[-1]
    rank, wt, cnt = pl.pallas_call(
        _router_kernel,
        out_shape=[jax.ShapeDtypeStruct((ne, n_tok), F32), jax.ShapeDtypeStruct((ne, n_tok), F32),
                   jax.ShapeDtypeStruct((nt, ne, LANES), jnp.int32)],
        grid=(nt,),
        in_specs=[pl.BlockSpec((tm, d), lambda i: (i, 0)), pl.BlockSpec((ne, d), lambda i: (0, 0)),
                  pl.BlockSpec((ne, 1), lambda i: (0, 0))],
        out_specs=[pl.BlockSpec((ne, tm), lambda i: (0, i)), pl.BlockSpec((ne, tm), lambda i: (0, i)),
                   pl.BlockSpec((1, ne, LANES), lambda i: (i, 0, 0))],
        compiler_params=pltpu.CompilerParams(dimension_semantics=("parallel",), vmem_limit_bytes=VMEM_LIMIT),
        name="moe_router",
    )(h, w_router.T.astype(F32), router_bias.astype(F32).reshape(ne, 1))
    counts = cnt[:, :, 0].reshape(-1)
    const = lambda i, e, c: (0, 0)
    once = pl.Buffered(1)
    return pl.pallas_call(
        _expert_kernel,
        out_shape=jax.ShapeDtypeStruct((n_tok, d), F32),
        grid_spec=pltpu.PrefetchScalarGridSpec(
            num_scalar_prefetch=1,
            grid=(nt, ne),
            in_specs=[pl.BlockSpec((tm, d), lambda i, e, c: (i, 0)),
                      pl.BlockSpec((1, 1, tm), lambda i, e, c: (e, 0, i)),
                      pl.BlockSpec((1, 1, tm), lambda i, e, c: (e, 0, i)),
                      pl.BlockSpec((1, d, ff), lambda i, e, c: (e, 0, 0)),
                      pl.BlockSpec((1, d, ff), lambda i, e, c: (e, 0, 0)),
                      pl.BlockSpec((1, ff, d), lambda i, e, c: (e, 0, 0)),
                      pl.BlockSpec(ws_gate.shape, const, pipeline_mode=once),
                      pl.BlockSpec(ws_up.shape, const, pipeline_mode=once),
                      pl.BlockSpec(ws_down.shape, const, pipeline_mode=once)],
            out_specs=pl.BlockSpec((tm, d), lambda i, e, c: (i, 0))),
        compiler_params=pltpu.CompilerParams(dimension_semantics=("parallel", "arbitrary"),
                                             vmem_limit_bytes=VMEM_LIMIT),
        name="moe_experts",
    )(counts, h, rank.reshape(ne, 1, n_tok), wt.reshape(ne, 1, n_tok),
      we_gate.astype(BF16), we_up.astype(BF16), we_down.astype(BF16),
      ws_gate.astype(BF16), ws_up.astype(BF16), ws_down.astype(BF16))


def _segments(t, n_ctx, fn):
    return jnp.concatenate([fn(t[:, :n_ctx]), fn(t[:, n_ctx:])], 1)


def _dw_conv(t, w):
    pad = w.shape[0] // 2
    return lax.conv_general_dilated(
        t, w[:, None, :].astype(t.dtype), window_strides=(1,), padding=((pad, pad),),
        dimension_numbers=('NWC', 'WIO', 'NWC'), feature_group_count=t.shape[-1])


def _l2n(t):
    return t * lax.rsqrt(jnp.sum(jnp.square(t), -1, keepdims=True) + RMS_EPS)


def _heads(t, n):
    return t.reshape(*t.shape[:-1], n, t.shape[-1] // n)


def _pad_cols(w, width):
    return jnp.pad(w, ((0, 0), (0, width - w.shape[1])))


def _pad_rows(w, height):
    return jnp.pad(w, ((0, height - w.shape[0]), (0, 0)))


def _mod_vectors(c, c_ctx, w_mod, b_mod):
    bn, d = c.shape
    cc = jnp.concatenate([c, c_ctx[None], jnp.zeros((8 - bn - 1, d), F32)], 0)
    mv = matmul([(jax.nn.silu(cc), w_mod.astype(BF16))]) + b_mod
    mv = mv.reshape(8, 6, d)
    lat = mv[:bn]
    ctx = jnp.broadcast_to(mv[bn][None], lat.shape)
    return jnp.stack([ctx, lat], 1)


def _pack_mv(gate, shift, scale):
    z = jnp.zeros_like(gate)
    return jnp.stack([gate, shift, scale, z, z, z, z, z], 2)


def _layer0_mixer(h, n_ctx, w_in, w_out, attn_sink, conv_w, a_log, dt_bias, norm_w, cs):
    bn, l_len, d = h.shape
    small = 3 * DN_HEADS
    w_main = A_Q + 2 * A_KV + 4 * DN_W
    w_in_p = jnp.concatenate([w_in[:, :w_main], _pad_cols(w_in[:, w_main:], LANES)], 1).astype(BF16)
    p = matmul([(h.reshape(bn * l_len, d), w_in_p)]).reshape(bn, l_len, -1)
    att = attention(p, cs, attn_sink, n_ctx, 0, A_Q, A_Q + A_KV)
    o_dn = A_Q + 2 * A_KV
    qkv = _segments(p[..., o_dn:o_dn + 3 * DN_W], n_ctx, lambda t: jax.nn.silu(_dw_conv(t, conv_w)))
    z = p[..., o_dn + 3 * DN_W:o_dn + 4 * DN_W]
    sm = p[..., w_main:w_main + small]
    a_f, a_b, bb = sm[..., :DN_HEADS], sm[..., DN_HEADS:2 * DN_HEADS], sm[..., 2 * DN_HEADS:]
    q, k, v = jnp.split(qkv, 3, -1)
    q = (_l2n(_heads(q, DN_HEADS)) * DN_HEAD_DIM ** -0.5).reshape(bn, l_len, DN_W)
    k = _l2n(_heads(k, DN_HEADS)).reshape(bn, l_len, DN_W)
    beta = jax.nn.sigmoid(bb)
    g = jnp.stack([-jnp.exp(a_log[0]) * jax.nn.softplus(a_f + dt_bias[0]),
                   -jnp.exp(a_log[1]) * jax.nn.softplus(a_b + dt_bias[1])], 0)
    o = dn_scan(q, k, v, beta, g, n_ctx)
    o = _heads(o[0] + o[1], DN_HEADS)
    o = o * lax.rsqrt(jnp.mean(jnp.square(o), -1, keepdims=True) + RMS_EPS) * norm_w
    dn = (o * jax.nn.silu(_heads(z, DN_HEADS))).reshape(bn, l_len, DN_W)
    w_out_b = w_out.astype(BF16)
    return matmul([(att.reshape(bn * l_len, A_Q), w_out_b[:A_Q]),
                   (dn.reshape(bn * l_len, DN_W), w_out_b[A_Q:])]).reshape(bn, l_len, d)


def _layer1_mixer(h, n_ctx, w_in, w_out, ssd_conv_w, ssd_conv_b, ssd_a_log, ssd_dt_bias, ssd_d, ssd_norm_w,
                  rw_mu, rw_w0, rw_w2, rw_a0, rw_a2, rw_g2, rw_k_k, rw_k_a, rw_r_k, rw_ln_g, rw_ln_b):
    bn, l_len, d = h.shape
    n_tok = bn * l_len
    p = matmul([(h.reshape(n_tok, d), _pad_cols(w_in, -(-w_in.shape[1] // 768) * 768).astype(BF16))])
    p = p.reshape(bn, l_len, -1)
    z, xs, bm, cm, dt_f, dt_b = jnp.split(
        p[..., :2 * SSD_W + 2 * SSD_GN + 2 * SSD_HEADS],
        np.cumsum([SSD_W, SSD_W, SSD_GN, SSD_GN, SSD_HEADS]).tolist(), -1)
    xbc = _segments(jnp.concatenate([xs, bm, cm], -1), n_ctx,
                    lambda t: jax.nn.silu(_dw_conv(t, ssd_conv_w) + ssd_conv_b))
    xs, bm, cm = xbc[..., :SSD_W], xbc[..., SSD_W:SSD_W + SSD_GN], xbc[..., SSD_W + SSD_GN:]
    dt = jnp.stack([jax.nn.softplus(dt_f + ssd_dt_bias[0]), jax.nn.softplus(dt_b + ssd_dt_bias[1])], 0)
    da = -jnp.exp(ssd_a_log)[:, None, None, :] * dt
    y = ssd_scan(cm, bm, xs, dt, da, n_ctx)
    y = _heads(y[0] + y[1], SSD_HEADS) + ssd_d[:, None] * _heads(xs, SSD_HEADS)
    y = y.reshape(bn, l_len, SSD_W) * jax.nn.silu(z)
    y = _heads(y, SSD_GROUPS)
    y = y * lax.rsqrt(jnp.mean(jnp.square(y), -1, keepdims=True) + RMS_EPS) * _heads(ssd_norm_w, SSD_GROUPS)
    ssd = y.reshape(bn, l_len, SSD_W)
    off = 2 * SSD_W + 2 * SSD_GN + 2 * SSD_HEADS
    rw_in = 3 * RW_W + 2 * RW_DECAY_LORA + RW_A_LORA + RW_GATE_LORA
    pr = p[..., off:off + rw_in]

    def shift(t):
        tp = jnp.pad(t, ((0, 0), (1, 1), (0, 0)))
        return 0.5 * (tp[:, :-2] + tp[:, 2:]) - t

    pr = pr + rw_mu * _segments(pr, n_ctx, shift)
    r, k, v, wl_f, wl_b, al, gl = jnp.split(
        pr, np.cumsum([RW_W, RW_W, RW_W, RW_DECAY_LORA, RW_DECAY_LORA, RW_A_LORA]).tolist(), -1)
    lora = lambda t, w: matmul([(_pad_cols(t.reshape(n_tok, -1), LANES), _pad_rows(w, LANES).astype(BF16))])
    a = jax.nn.sigmoid(rw_a0 + lora(al, rw_a2).reshape(bn, l_len, RW_W))
    g = matmul([(jax.nn.sigmoid(gl).reshape(n_tok, -1), rw_g2.astype(BF16))]).reshape(bn, l_len, RW_W)
    kk = _l2n(_heads(k * rw_k_k, RW_HEADS)).reshape(bn, l_len, RW_W)
    k = k * (1.0 + (a - 1.0) * rw_k_a)

    def log_decay(wl, i):
        w_log = -jax.nn.softplus(-(rw_w0[i] + lora(jnp.tanh(wl), rw_w2[i]).reshape(bn, l_len, RW_W))) - 0.5
        return -jnp.exp(w_log)

    lw = jnp.stack([log_decay(wl_f, 0), log_decay(wl_b, 1)], 0)
    y = rwkv_scan(r, k, v, kk, kk * a, lw, n_ctx)
    y = _heads(y[0] + y[1], RW_HEADS)
    mu = jnp.mean(y, -1, keepdims=True)
    var = jnp.mean(jnp.square(y - mu), -1, keepdims=True)
    y = (y - mu) * lax.rsqrt(var + RW_LN_EPS) * _heads(rw_ln_g, RW_HEADS) + _heads(rw_ln_b, RW_HEADS)
    rh, kh, vh = _heads(r, RW_HEADS), _heads(k, RW_HEADS), _heads(v, RW_HEADS)
    y = y + jnp.sum(rh * kh * rw_r_k, -1, keepdims=True) * vh
    rw = y.reshape(bn, l_len, RW_W) * g
    w_out_b = w_out.astype(BF16)
    return matmul([(ssd.reshape(n_tok, SSD_W), w_out_b[:SSD_W]),
                   (rw.reshape(n_tok, RW_W), w_out_b[SSD_W:])]).reshape(bn, l_len, d)


def kernel(x, c, ctx, c_ctx, l0_w_mod, l0_b_mod, l0_ln1_g, l0_ln1_b, l0_ln2_g, l0_ln2_b, l0_w_in, l0_w_out, l0_attn_sink, l0_dn_conv_w, l0_dn_a_log, l0_dn_dt_bias, l0_dn_norm_w, l0_w_router, l0_router_bias, l0_we_gate, l0_we_up, l0_we_down, l0_ws_gate, l0_ws_up, l0_ws_down, l1_w_mod, l1_b_mod, l1_ln1_g, l1_ln1_b, l1_ln2_g, l1_ln2_b, l1_w_in, l1_w_out, l1_ssd_conv_w, l1_ssd_conv_b, l1_ssd_a_log, l1_ssd_dt_bias, l1_ssd_d, l1_ssd_norm_w, l1_rw_mu, l1_rw_w0, l1_rw_w2, l1_rw_a0, l1_rw_a2, l1_rw_g2, l1_rw_k_k, l1_rw_k_a, l1_rw_r_k, l1_rw_ln_g, l1_rw_ln_b, l1_w_router, l1_router_bias, l1_we_gate, l1_we_up, l1_we_down, l1_ws_gate, l1_ws_up, l1_ws_down):
    bn, s_len, d = x.shape
    n_ctx = ctx.shape[1]
    l_len = n_ctx + s_len
    n_tok = bn * l_len
    z = jnp.concatenate([ctx, x], 1)
    cs = rope_table(n_ctx, s_len)
    mv0 = _mod_vectors(c, c_ctx, l0_w_mod, l0_b_mod)
    mv1 = _mod_vectors(c, c_ctx, l1_w_mod, l1_b_mod)
    pick = lambda mv, gi, shi, sci: _pack_mv(mv[:, :, gi], mv[:, :, shi], mv[:, :, sci])

    (h,) = ln_mod(z, z, pick(mv0, 2, 0, 1), l0_ln1_g, l0_ln1_b, n_ctx, do_ln=False, do_mod=True)
    o = _layer0_mixer(h, n_ctx, l0_w_in, l0_w_out, l0_attn_sink, l0_dn_conv_w, l0_dn_a_log, l0_dn_dt_bias,
                      l0_dn_norm_w, cs)
    z, h = ln_mod(z, o, pick(mv0, 2, 3, 4), l0_ln1_g, l0_ln1_b, n_ctx, do_ln=True, do_mod=True)
    f = moe(h.reshape(n_tok, d), l0_w_router, l0_router_bias, l0_we_gate, l0_we_up, l0_we_down,
            l0_ws_gate, l0_ws_up, l0_ws_down).reshape(bn, l_len, d)
    mv_a = _pack_mv(mv0[:, :, 5], mv1[:, :, 0], mv1[:, :, 1])
    z, h = ln_mod(z, f, mv_a, l0_ln2_g, l0_ln2_b, n_ctx, do_ln=True, do_mod=True)

    o = _layer1_mixer(h, n_ctx, l1_w_in, l1_w_out, l1_ssd_conv_w, l1_ssd_conv_b, l1_ssd_a_log, l1_ssd_dt_bias,
                      l1_ssd_d, l1_ssd_norm_w, l1_rw_mu, l1_rw_w0, l1_rw_w2, l1_rw_a0, l1_rw_a2, l1_rw_g2,
                      l1_rw_k_k, l1_rw_k_a, l1_rw_r_k, l1_rw_ln_g, l1_rw_ln_b)
    z, h = ln_mod(z, o, pick(mv1, 2, 3, 4), l1_ln1_g, l1_ln1_b, n_ctx, do_ln=True, do_mod=True)
    f = moe(h.reshape(n_tok, d), l1_w_router, l1_router_bias, l1_we_gate, l1_we_up, l1_we_down,
            l1_ws_gate, l1_ws_up, l1_ws_down).reshape(bn, l_len, d)
    (zx,) = ln_mod(z, f, pick(mv1, 5, 3, 4), l1_ln2_g, l1_ln2_b, n_ctx, do_ln=True, do_mod=False, lat_only=True)
    return zx
```

```python
import functools

import jax
import jax.numpy as jnp
import numpy as np
from jax import lax
from jax.experimental import pallas as pl
from jax.experimental.pallas import tpu as pltpu

F32 = jnp.float32
BF16 = jnp.bfloat16

DEPTH = 2
GRID_W = 64
HEAD_DIM = 128
A_Q_HEADS = 8
A_KV_HEADS = 2
WINDOW = 128
ATT_BLOCK = 128
ROPE_THETA = 10000.0
DN_HEADS = 8
DN_HEAD_DIM = 128
SSD_HEADS = 32
SSD_HEAD_DIM = 64
SSD_GROUPS = 2
SSD_STATE = 128
RW_HEADS = 32
RW_HEAD_DIM = 64
RW_DECAY_LORA = 96
RW_A_LORA = 96
RW_GATE_LORA = 256
RW_LN_EPS = 64e-5
N_EXPERTS = 64
TOP_K = 8
N_GROUPS = 8
TOPK_GROUPS = 4
ROUTED_SCALE = 2.5
LN_EPS = 1e-5
RMS_EPS = 1e-6
DEEPNORM_ALPHA = (2 * DEPTH) ** 0.25

A_Q = A_Q_HEADS * HEAD_DIM
A_KV = A_KV_HEADS * HEAD_DIM
DN_W = DN_HEADS * DN_HEAD_DIM
SSD_W = SSD_HEADS * SSD_HEAD_DIM
SSD_GN = SSD_GROUPS * SSD_STATE
RW_W = RW_HEADS * RW_HEAD_DIM

LANES = 128
SCAN_CHUNK = 64
VMEM_LIMIT = 56 * 1024 * 1024
NEG_BIG = -1e30

HIGHEST = lax.Precision.HIGHEST


def _dot_dims(a, b, dims, precision):
    if precision is None:
        a, b = a.astype(BF16), b.astype(BF16)
    return lax.dot_general(a, b, (dims, ((), ())), preferred_element_type=F32, precision=precision)


def _dot(a, b, precision=None):
    return _dot_dims(a, b, ((1,), (0,)), precision)


def _dot_nt(a, b, precision=None):
    return _dot_dims(a, b, ((1,), (1,)), precision)


def _dot_tn(a, b, precision=None):
    return _dot_dims(a, b, ((0,), (0,)), precision)


def _pick_tile(n, candidates):
    for c in candidates:
        if n % c == 0:
            return c
    return n


def _mm_kernel(*refs, n_pairs):
    o_ref = refs[2 * n_pairs]
    acc = None
    for a_ref, w_ref in zip(refs[:n_pairs], refs[n_pairs:2 * n_pairs]):
        p = _dot(a_ref[...].astype(BF16), w_ref[...].astype(BF16))
        acc = p if acc is None else acc + p
    o_ref[...] = acc.astype(o_ref.dtype)


def matmul(pairs, out_dtype=F32):
    m = pairs[0][0].shape[0]
    n = pairs[0][1].shape[1]
    tn = _pick_tile(n, (1280, 1152, 1024, 768, 512, 384, 256, 128))
    tm = _pick_tile(m, (1024, 512, 256, 128) if tn <= 768 else (512, 256, 128))
    in_specs = ([pl.BlockSpec((tm, a.shape[1]), lambda j, i: (i, 0)) for a, _ in pairs]
                + [pl.BlockSpec((w.shape[0], tn), lambda j, i: (0, j)) for _, w in pairs])
    return pl.pallas_call(
        functools.partial(_mm_kernel, n_pairs=len(pairs)),
        out_shape=jax.ShapeDtypeStruct((m, n), out_dtype),
        grid=(n // tn, m // tm),
        in_specs=in_specs,
        out_specs=pl.BlockSpec((tm, tn), lambda j, i: (i, j)),
        compiler_params=pltpu.CompilerParams(dimension_semantics=("parallel", "parallel"),
                                             vmem_limit_bytes=VMEM_LIMIT),
        name="matmul",
    )(*[a for a, _ in pairs], *[w for _, w in pairs])


def _residual_ln_mod(z, o, mv, g, b, zn_ref, h_ref):
    t = DEEPNORM_ALPHA * z + mv[0:1] * o
    tc = t - jnp.mean(t, -1, keepdims=True)
    zn = tc * lax.rsqrt(jnp.mean(tc * tc, -1, keepdims=True) + LN_EPS) * g + b
    zn_ref[0] = zn
    h_ref[0] = (zn * (1.0 + mv[2:3]) + mv[1:2]).astype(h_ref.dtype)


def _ln_mod_kernel(z_ref, o_ref, mv_ref, g_ref, b_ref, zn_ref, h_ref, *, do_ln, do_mod):
    z = z_ref[0]
    mv = mv_ref[0, 0]
    if do_ln:
        t = DEEPNORM_ALPHA * z + mv[0:1] * o_ref[0]
        mu = jnp.mean(t, -1, keepdims=True)
        tc = t - mu
        var = jnp.mean(tc * tc, -1, keepdims=True)
        z = tc * lax.rsqrt(var + LN_EPS) * g_ref[...] + b_ref[...]
        zn_ref[0] = z
    if do_mod:
        h_ref[0] = (z * (1.0 + mv[2:3]) + mv[1:2]).astype(h_ref.dtype)


def ln_mod(z, o, mv, ln_g, ln_b, n_ctx, *, do_ln, do_mod, lat_only=False):
    bn, l_len, d = z.shape
    tr = 256
    nbc = n_ctx // tr
    off = nbc if lat_only else 0
    n_out = l_len - off * tr
    row = lambda b, i: (b, i + off, 0)
    outs, out_specs = [], []
    if do_ln:
        outs.append(jax.ShapeDtypeStruct((bn, n_out, d), F32))
        out_specs.append(pl.BlockSpec((1, tr, d), lambda b, i: (b, i, 0)))
    if do_mod:
        outs.append(jax.ShapeDtypeStruct((bn, n_out, d), BF16))
        out_specs.append(pl.BlockSpec((1, tr, d), lambda b, i: (b, i, 0)))

    def body(z_ref, o_ref, mv_ref, g_ref, b_ref, *out_refs):
        zn_ref = out_refs[0] if do_ln else None
        h_ref = out_refs[-1] if do_mod else None
        _ln_mod_kernel(z_ref, o_ref, mv_ref, g_ref, b_ref, zn_ref, h_ref, do_ln=do_ln, do_mod=do_mod)

    res = pl.pallas_call(
        body,
        out_shape=outs,
        grid=(bn, n_out // tr),
        in_specs=[pl.BlockSpec((1, tr, d), row), pl.BlockSpec((1, tr, d), row),
                  pl.BlockSpec((1, 1, 8, d), lambda b, i: (b, jnp.where(i + off < nbc, 0, 1), 0, 0)),
                  pl.BlockSpec((1, d), lambda b, i: (0, 0)), pl.BlockSpec((1, d), lambda b, i: (0, 0))],
        out_specs=out_specs,
        compiler_params=pltpu.CompilerParams(dimension_semantics=("parallel", "parallel"),
                                             vmem_limit_bytes=VMEM_LIMIT),
        name="ln_mod",
    )(z, o, mv, ln_g.reshape(1, d), ln_b.reshape(1, d))
    return res


def _rope(x, cs):
    lane = lax.broadcasted_iota(jnp.int32, x.shape, 1)
    quarter = HEAD_DIM // 4
    first = (lane % (2 * quarter)) < quarter
    partner = jnp.where(first, pltpu.roll(x, HEAD_DIM - quarter, 1), pltpu.roll(x, quarter, 1))
    return x * cs[:, :HEAD_DIM] + partner * cs[:, HEAD_DIM:]


def _attn_kernel(q_ref, kp_ref, kc_ref, kn_ref, vp_ref, vc_ref, vn_ref, kx_ref, vx_ref,
                 csp_ref, csc_ref, csn_ref, sink_ref, o_ref, *, n_ctx_blocks, s_len):
    i = pl.program_id(1)
    blk = ATT_BLOCK
    grp = A_Q_HEADS // A_KV_HEADS
    n_ctx = kx_ref.shape[1]
    csc = csc_ref[...]
    q = q_ref[0]
    q4 = jnp.concatenate([_rope(q[:, h * HEAD_DIM:(h + 1) * HEAD_DIM], csc) for h in range(grp)], 0)
    k_all = jnp.concatenate([_rope(kp_ref[0], csp_ref[...]), _rope(kc_ref[0], csc),
                             _rope(kn_ref[0], csn_ref[...]), kx_ref[0]], 0)
    v_all = jnp.concatenate([vp_ref[0], vc_ref[0], vn_ref[0], vx_ref[0]], 0)
    n_keys = 3 * blk + n_ctx
    s = _dot_nt(q4, k_all) * HEAD_DIM ** -0.5
    s = s.reshape(grp, blk, n_keys)
    tq = lax.broadcasted_iota(jnp.int32, (1, blk, n_keys), 1)
    col = lax.broadcasted_iota(jnp.int32, (1, blk, n_keys), 2)
    koff = col - blk
    pos = (i - n_ctx_blocks) * blk + koff
    local_ok = (jnp.abs(tq - koff) <= WINDOW) & (pos >= 0) & (pos < s_len) & (i >= n_ctx_blocks)
    valid = local_ok | (col >= 3 * blk)
    s = jnp.where(valid, s, -jnp.inf)
    sink = sink_ref[0]
    hsel = lax.broadcasted_iota(jnp.int32, (grp, 1, 1), 0)
    sink3 = jnp.zeros((grp, 1, 1), F32)
    for h in range(grp):
        sink3 = jnp.where(hsel == h, sink[:, h:h + 1].reshape(1, 1, 1), sink3)
    m = jnp.maximum(jnp.max(s, -1, keepdims=True), sink3)
    p = jnp.exp(s - m)
    denom = jnp.sum(p, -1, keepdims=True) + jnp.exp(sink3 - m)
    o = _dot(p.reshape(grp * blk, n_keys), v_all).reshape(grp, blk, HEAD_DIM) / denom
    for h in range(grp):
        o_ref[0, :, h * HEAD_DIM:(h + 1) * HEAD_DIM] = o[h]


def attention(p_all, cs, sink, n_ctx, q_col, k_col, v_col):
    bn, l_len, _ = p_all.shape
    blk = ATT_BLOCK
    grp = A_Q_HEADS // A_KV_HEADS
    nb = l_len // blk
    ncb = n_ctx // blk
    s_len = l_len - n_ctx
    qw = grp * HEAD_DIM
    prev = lambda i: jnp.clip(i - 1, ncb, nb - 1)
    cur = lambda i: jnp.clip(i, ncb, nb - 1)
    nxt = lambda i: jnp.clip(i + 1, ncb, nb - 1)
    kb, vb = k_col // HEAD_DIM, v_col // HEAD_DIM

    def kv_spec(colb, rowf):
        return pl.BlockSpec((1, blk, HEAD_DIM), lambda b, i, h: (b, rowf(i), colb + h))

    def cs_spec(rowf):
        return pl.BlockSpec((blk, 2 * HEAD_DIM), lambda b, i, h: (rowf(i), 0))

    return pl.pallas_call(
        functools.partial(_attn_kernel, n_ctx_blocks=ncb, s_len=s_len),
        out_shape=jax.ShapeDtypeStruct((bn, l_len, A_Q), F32),
        grid=(bn, nb, A_KV_HEADS),
        in_specs=[pl.BlockSpec((1, blk, qw), lambda b, i, h: (b, i, q_col // qw + h)),
                  kv_spec(kb, prev), kv_spec(kb, cur), kv_spec(kb, nxt),
                  kv_spec(vb, prev), kv_spec(vb, cur), kv_spec(vb, nxt),
                  pl.BlockSpec((1, n_ctx, HEAD_DIM), lambda b, i, h: (b, 0, kb + h)),
                  pl.BlockSpec((1, n_ctx, HEAD_DIM), lambda b, i, h: (b, 0, vb + h)),
                  cs_spec(prev), pl.BlockSpec((blk, 2 * HEAD_DIM), lambda b, i, h: (i, 0)), cs_spec(nxt),
                  pl.BlockSpec((1, 1, grp), lambda b, i, h: (h, 0, 0))],
        out_specs=pl.BlockSpec((1, blk, qw), lambda b, i, h: (b, i, h)),
        compiler_params=pltpu.CompilerParams(dimension_semantics=("parallel", "parallel", "parallel"),
                                             vmem_limit_bytes=VMEM_LIMIT),
        name="window_attention",
    )(p_all, p_all, p_all, p_all, p_all, p_all, p_all, p_all, p_all, cs, cs, cs,
      sink.astype(F32).reshape(A_KV_HEADS, 1, grp))


def rope_table(n_ctx, s_len):
    rows = s_len // GRID_W
    row = jnp.repeat(jnp.arange(rows, dtype=F32), GRID_W)
    colp = jnp.tile(jnp.arange(GRID_W, dtype=F32), rows)
    n_freq = HEAD_DIM // 4
    inv_freq = ROPE_THETA ** (-jnp.arange(n_freq, dtype=F32) / n_freq)
    ang_r = row[:, None] * inv_freq
    ang_c = colp[:, None] * inv_freq
    cos = jnp.concatenate([jnp.cos(ang_r), jnp.cos(ang_r), jnp.cos(ang_c), jnp.cos(ang_c)], -1)
    sin = jnp.concatenate([-jnp.sin(ang_r), jnp.sin(ang_r), -jnp.sin(ang_c), jnp.sin(ang_c)], -1)
    lat = jnp.concatenate([cos, sin], -1)
    ctx = jnp.concatenate([jnp.ones((n_ctx, HEAD_DIM), F32), jnp.zeros((n_ctx, HEAD_DIM), F32)], -1)
    return jnp.concatenate([ctx, lat], 0)


def _order_masks(c, d):
    t = lax.broadcasted_iota(jnp.int32, (c, c), 0)
    j = lax.broadcasted_iota(jnp.int32, (c, c), 1)
    rel = (t - j) * (1 - 2 * d)
    return rel >= 0, rel > 0


def _unit_inverses(ms):
    c = ms[0].shape[0]
    eye = (lax.broadcasted_iota(jnp.int32, (c, c), 0) == lax.broadcasted_iota(jnp.int32, (c, c), 1)).astype(F32)
    ts = [eye + m for m in ms]
    pws = [_dot(m, m) for m in ms]
    n = 2
    while 2 * n < c:
        st = [_dot(jnp.concatenate([t, p], 0), p) for t, p in zip(ts, pws)]
        ts = [t + s[:c] for t, s in zip(ts, st)]
        pws = [s[c:] for s in st]
        n *= 2
    return [t + _dot(t, p) for t, p in zip(ts, pws)]


def _scan_chunk(hts, r_t, a_t, v, k_h, b_h, pc, n_rk, n_rb=None, m_ab=None, m_ak=None):
    zipm = lambda f, *ls: [f(*xs) for xs in zip(*ls)]
    if m_ab is None:
        y0 = zipm(_dot, n_rk, v)
        s0 = zipm(_dot_tn, v, k_h)
        ys = zipm(lambda r, ht, y: _dot_nt(r, ht) + y, r_t, hts, y0)
        return ys, zipm(lambda ht, p, s: ht * p + s, hts, pc, s0)
    dk = a_t[0].shape[1]
    ts = _unit_inverses(m_ab)
    mv = zipm(_dot, m_ak, v)
    wu = zipm(lambda t, a, m: _dot(t, jnp.concatenate([a, m], 1)), ts, a_t, mv)
    nwu = zipm(_dot, n_rb, wu)
    nkv = zipm(_dot, n_rk, v)
    gs = zipm(_dot_tn, wu, b_h)
    vk = zipm(_dot_tn, v, k_h)
    ys = zipm(lambda r, n, ht, y: _dot_nt(r + n[:, :dk], ht) + n[:, dk:] + y, r_t, nwu, hts, nkv)
    return ys, zipm(lambda ht, p, g_, s: ht * p + _dot(ht, g_[:dk]) + g_[dk:] + s, hts, pc, gs, vk)


def _time_block(d, j, ncc, nct):
    bwd = jnp.where(j < ncc, ncc - 1 - j, nct - 1 - j + ncc)
    return jnp.where(d == 0, j, bwd)


def _cum_scalars(col_g, row_g, incl):
    tri = incl.astype(F32)
    cum_col = _dot(tri, col_g, HIGHEST)
    cum_row = _dot_nt(row_g, tri, HIGHEST)
    tot = jnp.sum(col_g, 0, keepdims=True)
    return cum_col, cum_row, tot


def _dn_chunk_args(q_ref, k_ref, v_ref, col_ref, row_ref, d, heads, dh):
    c = q_ref.shape[1]
    incl, strict = _order_masks(c, d)
    col = col_ref[0, 0]
    row = row_ref[0, 0, 0]
    cum_col, cum_row, tot = _cum_scalars(col[:, heads:], row[heads:], incl)
    hs = range(heads)
    sls = [slice(h * dh, (h + 1) * dh) for h in hs]
    qs, ks, vs = ([ref[0, :, sl] for sl in sls] for ref in (q_ref, k_ref, v_ref))
    kks = [_dot_nt(k, k) for k in ks]
    qks = [_dot_nt(q, k) for q, k in zip(qs, ks)]
    args = [[] for _ in range(10)]
    for h in hs:
        q, k = qs[h], ks[h]
        bc, gc = col[:, h:h + 1], col[:, heads + h:heads + h + 1]
        br, gr = row[h:h + 1], row[heads + h:heads + h + 1]
        cc, cr, tt = cum_col[:, h:h + 1], cum_row[h:h + 1], tot[:, h:h + 1]
        kkb = kks[h] * br
        qkb = qks[h] * br
        dif = cc - cr
        m_ak = kkb * jnp.exp(jnp.where(strict, dif - gc, NEG_BIG))
        m_ab = -kkb * jnp.exp(jnp.where(strict, dif - gc + gr, NEG_BIG))
        n_rk = qkb * jnp.exp(jnp.where(incl, dif, NEG_BIG))
        n_rb = -qkb * jnp.exp(jnp.where(incl, dif + gr, NEG_BIG))
        a_t = k * jnp.exp(cc - gc)
        r_t = q * jnp.exp(cc)
        k_h = k * (bc * jnp.exp(tt - cc))
        b_h = -k * (bc * jnp.exp(gc + tt - cc))
        for lst, val in zip(args, (r_t, a_t, vs[h], k_h, b_h, jnp.exp(tt), n_rk, n_rb, m_ab, m_ak)):
            lst.append(val)
    return args


def _dn_scan_kernel(qf_ref, kf_ref, vf_ref, colf_ref, rowf_ref, qb_ref, kb_ref, vb_ref, colb_ref, rowb_ref,
                    of_ref, ob_ref, st_ref, *, heads, dh):
    @pl.when(pl.program_id(1) == 0)
    def _():
        st_ref[...] = jnp.zeros_like(st_ref)

    fwd = _dn_chunk_args(qf_ref, kf_ref, vf_ref, colf_ref, rowf_ref, 0, heads, dh)
    bwd = _dn_chunk_args(qb_ref, kb_ref, vb_ref, colb_ref, rowb_ref, 1, heads, dh)
    ys, hts = _scan_chunk([st_ref[h] for h in range(2 * heads)], *[f + b for f, b in zip(fwd, bwd)])
    for h in range(2 * heads):
        st_ref[h] = hts[h]
    for h in range(heads):
        of_ref[0, :, h * dh:(h + 1) * dh] = ys[h]
        ob_ref[0, :, h * dh:(h + 1) * dh] = ys[heads + h]


def dn_scan(q, k, v, beta, g, n_ctx, col_blocks=(0, 0, 0)):
    bn, l_len, _ = q.shape
    heads = beta.shape[-1]
    dh = DN_HEAD_DIM
    w = heads * dh
    c = SCAN_CHUNK
    nct, ncc = l_len // c, n_ctx // c
    col = jnp.concatenate([jnp.broadcast_to(beta[None], g.shape), g], -1)
    row = jnp.swapaxes(col.reshape(2, bn, nct, c, 2 * heads), 3, 4)
    tb = lambda d, j: _time_block(d, j, ncc, nct)
    seq = lambda d, cb: pl.BlockSpec((1, c, w), lambda b, j: (b, tb(d, j), cb))
    side = lambda d: [seq(d, col_blocks[0]), seq(d, col_blocks[1]), seq(d, col_blocks[2]),
                      pl.BlockSpec((1, 1, c, 2 * heads), lambda b, j: (d, b, tb(d, j), 0)),
                      pl.BlockSpec((1, 1, 1, 2 * heads, c), lambda b, j: (d, b, tb(d, j), 0, 0))]
    out = jax.ShapeDtypeStruct((bn, l_len, w), F32)
    return pl.pallas_call(
        functools.partial(_dn_scan_kernel, heads=heads, dh=dh),
        out_shape=[out, out],
        grid=(bn, nct),
        in_specs=side(0) + side(1),
        out_specs=[pl.BlockSpec((1, c, w), lambda b, j: (b, tb(0, j), 0)),
                   pl.BlockSpec((1, c, w), lambda b, j: (b, tb(1, j), 0))],
        scratch_shapes=[pltpu.VMEM((2 * heads, dh, dh), F32)],
        compiler_params=pltpu.CompilerParams(dimension_semantics=("parallel", "arbitrary")),
        name="dn_scan",
    )(q, k, v, col, row, q, k, v, col, row)


def _ssd_scan_kernel(c_ref, b_ref, x_ref, col_ref, row_ref, o_ref, st_ref, *, heads, groups, dh, n_st):
    d = pl.program_id(1)

    @pl.when(pl.program_id(2) == 0)
    def _():
        st_ref[...] = jnp.zeros_like(st_ref)

    c = c_ref.shape[1]
    incl, _ = _order_masks(c, d)
    col = col_ref[0, 0]
    row = row_ref[0, 0, 0]
    cum_col, cum_row, tot = _cum_scalars(col[:, heads:], row, incl)
    hg = heads // groups
    cms = [c_ref[0, :, g * n_st:(g + 1) * n_st] for g in range(groups)]
    bms = [b_ref[0, :, g * n_st:(g + 1) * n_st] for g in range(groups)]
    cbs = [_dot_nt(cm, bm) for cm, bm in zip(cms, bms)]
    hs = range(heads)
    sls = [slice(h * dh, (h + 1) * dh) for h in hs]
    args = [[] for _ in range(7)]
    for h in hs:
        g = h // hg
        cc, cr, tt = cum_col[:, h:h + 1], cum_row[h:h + 1], tot[:, h:h + 1]
        v = x_ref[0, :, sls[h]] * col[:, h:h + 1]
        n_rk = cbs[g] * jnp.exp(jnp.where(incl, cc - cr, NEG_BIG))
        vals = (cms[g] * jnp.exp(cc), None, v, bms[g] * jnp.exp(tt - cc), None, jnp.exp(tt), n_rk)
        for lst, val in zip(args, vals):
            lst.append(val)
    ys, hts = _scan_chunk([st_ref[h] for h in hs], *args)
    for h in hs:
        st_ref[h] = hts[h]
        o_ref[0, 0, :, sls[h]] = ys[h]


def ssd_scan(cm, bm, x, dt, da, n_ctx, col_blocks=(0, 0, 0)):
    bn, l_len, _ = x.shape
    heads = dt.shape[-1]
    dh, n_st, grp = SSD_HEAD_DIM, SSD_STATE, SSD_GROUPS
    cb0, bb0, xb0 = col_blocks
    c = SCAN_CHUNK
    nct, ncc = l_len // c, n_ctx // c
    col = jnp.concatenate([dt, da], -1)
    row = jnp.swapaxes(da.reshape(2, bn, nct, c, heads), 3, 4)
    tb = lambda d, j: _time_block(d, j, ncc, nct)
    gseq = lambda off: pl.BlockSpec((1, c, grp * n_st), lambda b, d, j: (b, tb(d, j), off))
    return pl.pallas_call(
        functools.partial(_ssd_scan_kernel, heads=heads, groups=grp, dh=dh, n_st=n_st),
        out_shape=jax.ShapeDtypeStruct((2, bn, l_len, heads * dh), F32),
        grid=(bn, 2, nct),
        in_specs=[gseq(cb0), gseq(bb0),
                  pl.BlockSpec((1, c, heads * dh), lambda b, d, j: (b, tb(d, j), xb0)),
                  pl.BlockSpec((1, 1, c, 2 * heads), lambda b, d, j: (d, b, tb(d, j), 0)),
                  pl.BlockSpec((1, 1, 1, heads, c), lambda b, d, j: (d, b, tb(d, j), 0, 0))],
        out_specs=pl.BlockSpec((1, 1, c, heads * dh), lambda b, d, j: (d, b, tb(d, j), 0)),
        scratch_shapes=[pltpu.VMEM((heads, dh, n_st), F32)],
        compiler_params=pltpu.CompilerParams(dimension_semantics=("parallel", "parallel", "arbitrary")),
        name="ssd_scan",
    )(cm, bm, x, col, row)


def _rwkv_scan_kernel(r_ref, k_ref, v_ref, kk_ref, b_ref, lw_ref, o_ref, st_ref, *, heads, dh):
    d = pl.program_id(2)

    @pl.when(pl.program_id(3) == 0)
    def _():
        st_ref[...] = jnp.zeros_like(st_ref)

    c = r_ref.shape[1]
    incl, strict = _order_masks(c, d)
    lw_all = lw_ref[0, 0]
    cl_all = _dot(incl.astype(F32), lw_all, HIGHEST)
    tot_all = jnp.sum(lw_all, 0, keepdims=True)
    hs = range(heads)
    sls = [slice(h * dh, (h + 1) * dh) for h in hs]
    args = [[] for _ in range(6)]
    grams = []
    for h in hs:
        sl = sls[h]
        r, k, v, kk, b = r_ref[0, :, sl], k_ref[0, :, sl], v_ref[0, :, sl], kk_ref[0, :, sl], b_ref[0, :, sl]
        lw, cl, tt = lw_all[:, sl], cl_all[:, sl], tot_all[:, sl]
        inv_p = jnp.exp(-cl)
        w_end = jnp.exp(tt - cl)
        a_t = -kk * jnp.exp(cl - lw)
        r_t = r * jnp.exp(cl)
        grams.append((a_t, r_t, b * inv_p, k * inv_p))
        for lst, val in zip(args, (r_t, a_t, v, k * w_end, b * w_end, jnp.exp(tt))):
            lst.append(val)
    gm = [_dot_nt(jnp.concatenate([a, r], 0), jnp.concatenate([b, k], 0)) for a, r, b, k in grams]
    m_ab = [jnp.where(strict, t[:c, :c], 0.0) for t in gm]
    m_ak = [jnp.where(strict, t[:c, c:], 0.0) for t in gm]
    n_rb = [jnp.where(incl, t[c:, :c], 0.0) for t in gm]
    n_rk = [jnp.where(incl, t[c:, c:], 0.0) for t in gm]
    ys, hts = _scan_chunk([st_ref[h] for h in hs], *args, n_rk, n_rb, m_ab, m_ak)
    for h in hs:
        st_ref[h] = hts[h]
        o_ref[0, 0, :, sls[h]] = ys[h]


def rwkv_scan(r, k, v, kk, b, lw, n_ctx, heads_per_step=32):
    bn, l_len, w = r.shape
    dh = RW_HEAD_DIM
    hb = heads_per_step
    c = SCAN_CHUNK
    nct, ncc = l_len // c, n_ctx // c
    tb = lambda d, j: _time_block(d, j, ncc, nct)
    seq = pl.BlockSpec((1, c, hb * dh), lambda b_, g, d, j: (b_, tb(d, j), g))
    return pl.pallas_call(
        functools.partial(_rwkv_scan_kernel, heads=hb, dh=dh),
        out_shape=jax.ShapeDtypeStruct((2, bn, l_len, w), F32),
        grid=(bn, w // (hb * dh), 2, nct),
        in_specs=[seq, seq, seq, seq, seq,
                  pl.BlockSpec((1, 1, c, hb * dh), lambda b_, g, d, j: (d, b_, tb(d, j), g))],
        out_specs=pl.BlockSpec((1, 1, c, hb * dh), lambda b_, g, d, j: (d, b_, tb(d, j), g)),
        scratch_shapes=[pltpu.VMEM((hb, dh, dh), F32)],
        compiler_params=pltpu.CompilerParams(
            dimension_semantics=("parallel", "parallel", "parallel", "arbitrary")),
        name="rwkv_scan",
    )(r, k, v, kk, b, lw)


def _router_kernel(h_ref, wr_ref, bias_ref, rank_ref, wt_ref, cnt_ref):
    tm = h_ref.shape[0]
    ne, ng = N_EXPERTS, N_GROUPS
    per = ne // ng
    logits = _dot_nt(wr_ref[...], h_ref[...].astype(F32), HIGHEST)
    scores = jax.nn.sigmoid(logits)
    sel = scores + bias_ref[...]
    sel3 = sel.reshape(ng, per, tm)
    idx3 = lax.broadcasted_iota(jnp.int32, (ng, per, tm), 1)
    m1 = jnp.max(sel3, 1, keepdims=True)
    first = jnp.min(jnp.where(sel3 == m1, idx3, per), 1, keepdims=True)
    m2 = jnp.max(jnp.where(idx3 == first, -jnp.inf, sel3), 1, keepdims=True)
    grp = (m1 + m2).reshape(ng, tm)
    gidx = lax.broadcasted_iota(jnp.int32, (ng, tm), 0)
    chosen = jnp.zeros((ng, tm), jnp.bool_)
    for _ in range(TOPK_GROUPS):
        gm = jnp.max(grp, 0, keepdims=True)
        gi = jnp.min(jnp.where(grp == gm, gidx, ng), 0, keepdims=True)
        hit = gidx == gi
        chosen = chosen | hit
        grp = jnp.where(hit, -jnp.inf, grp)
    selm = jnp.where(chosen.reshape(ng, 1, tm), sel3, -jnp.inf).reshape(ne, tm)
    eidx = lax.broadcasted_iota(jnp.int32, (ne, tm), 0)
    picked = jnp.zeros((ne, tm), jnp.bool_)
    for _ in range(TOP_K):
        em = jnp.max(selm, 0, keepdims=True)
        ei = jnp.min(jnp.where(selm == em, eidx, ne), 0, keepdims=True)
        hit = eidx == ei
        picked = picked | hit
        selm = jnp.where(hit, -jnp.inf, selm)
    wsel = jnp.where(picked, scores, 0.0)
    wt_ref[...] = wsel / jnp.sum(wsel, 0, keepdims=True) * ROUTED_SCALE
    onehot = picked.astype(BF16)
    before = (lax.broadcasted_iota(jnp.int32, (tm, tm), 0) < lax.broadcasted_iota(jnp.int32, (tm, tm), 1))
    rank = _dot(onehot, before.astype(BF16))
    rank_ref[...] = jnp.where(picked, rank, -1.0)
    cnt = jnp.sum(picked.astype(F32), 1, keepdims=True)
    cnt_ref[0] = jnp.broadcast_to(cnt, (ne, LANES)).astype(jnp.int32)


def _expert_kernel(cnt_ref, h_ref, rank_ref, wt_ref, wg_ref, wu_ref, wd_ref, sg_ref, su_ref, sd_ref, o_ref):
    i, e = pl.program_id(0), pl.program_id(1)
    tm = h_ref.shape[0]
    rows = 128
    h = h_ref[...]

    d = o_ref.shape[1]
    cw = 512

    @pl.when(e == 0)
    def _():
        act = (jax.nn.silu(_dot(h, sg_ref[...])) * _dot(h, su_ref[...])).astype(BF16)
        for c0 in range(0, d, cw):
            o_ref[:, c0:c0 + cw] = _dot(act, sd_ref[:, c0:c0 + cw])

    rk = rank_ref[0]
    wt = wt_ref[0]
    n_blk = (cnt_ref[i * N_EXPERTS + e] + rows - 1) // rows
    r_iota = lax.broadcasted_iota(jnp.int32, (rows, tm), 0).astype(F32)

    def body(b, carry):
        hit = rk == r_iota + (b * rows).astype(F32)
        xg = _dot(hit.astype(BF16), h).astype(BF16)
        act = jax.nn.silu(_dot(xg, wg_ref[0])) * _dot(xg, wu_ref[0])
        y = _dot(act.astype(BF16), wd_ref[0]).astype(BF16)
        gw = jnp.where(hit, wt, 0.0).astype(BF16)
        for c0 in range(0, d, cw):
            o_ref[:, c0:c0 + cw] += _dot_tn(gw, y[:, c0:c0 + cw])
        return carry

    lax.fori_loop(0, n_blk, body, 0)


def moe(h, w_router, router_bias, we_gate, we_up, we_down, ws_gate, ws_up, ws_down, tm=768):
    n_tok, d = h.shape
    ne = N_EXPERTS
    nt = n_tok // tm
    ff = we_gate.shape[-1]
    rank, wt, cnt = pl.pallas_call(
        _router_kernel,
        out_shape=[jax.ShapeDtypeStruct((ne, n_tok), F32), jax.ShapeDtypeStruct((ne, n_tok), F32),
                   jax.ShapeDtypeStruct((nt, ne, LANES), jnp.int32)],
        grid=(nt,),
        in_specs=[pl.BlockSpec((tm, d), lambda i: (i, 0)), pl.BlockSpec((ne, d), lambda i: (0, 0)),
                  pl.BlockSpec((ne, 1), lambda i: (0, 0))],
        out_specs=[pl.BlockSpec((ne, tm), lambda i: (0, i)), pl.BlockSpec((ne, tm), lambda i: (0, i)),
                   pl.BlockSpec((1, ne, LANES), lambda i: (i, 0, 0))],
        compiler_params=pltpu.CompilerParams(dimension_semantics=("parallel",), vmem_limit_bytes=VMEM_LIMIT),
        name="moe_router",
    )(h, w_router.T.astype(F32), router_bias.astype(F32).reshape(ne, 1))
    counts = cnt[:, :, 0].reshape(-1)
    const = lambda i, e, c: (0, 0)
    once = pl.Buffered(1)
    return pl.pallas_call(
        _expert_kernel,
        out_shape=jax.ShapeDtypeStruct((n_tok, d), F32),
        grid_spec=pltpu.PrefetchScalarGridSpec(
            num_scalar_prefetch=1,
            grid=(nt, ne),
            in_specs=[pl.BlockSpec((tm, d), lambda i, e, c: (i, 0)),
                      pl.BlockSpec((1, 1, tm), lambda i, e, c: (e, 0, i)),
                      pl.BlockSpec((1, 1, tm), lambda i, e, c: (e, 0, i)),
                      pl.BlockSpec((1, d, ff), lambda i, e, c: (e, 0, 0)),
                      pl.BlockSpec((1, d, ff), lambda i, e, c: (e, 0, 0)),
                      pl.BlockSpec((1, ff, d), lambda i, e, c: (e, 0, 0)),
                      pl.BlockSpec(ws_gate.shape, const, pipeline_mode=once),
                      pl.BlockSpec(ws_up.shape, const, pipeline_mode=once),
                      pl.BlockSpec(ws_down.shape, const, pipeline_mode=once)],
            out_specs=pl.BlockSpec((tm, d), lambda i, e, c: (i, 0))),
        compiler_params=pltpu.CompilerParams(dimension_semantics=("parallel", "arbitrary"),
                                             vmem_limit_bytes=VMEM_LIMIT),
        name="moe_experts",
    )(counts, h, rank.reshape(ne, 1, n_tok), wt.reshape(ne, 1, n_tok),
      we_gate.astype(BF16), we_up.astype(BF16), we_down.astype(BF16),
      ws_gate.astype(BF16), ws_up.astype(BF16), ws_down.astype(BF16))


def _segments(t, n_ctx, fn):
    return jnp.concatenate([fn(t[:, :n_ctx]), fn(t[:, n_ctx:])], 1)


def _dw_conv(t, w):
    pad = w.shape[0] // 2
    return lax.conv_general_dilated(
        t, w[:, None, :].astype(t.dtype), window_strides=(1,), padding=((pad, pad),),
        dimension_numbers=('NWC', 'WIO', 'NWC'), feature_group_count=t.shape[-1])


def _l2n(t):
    return t * lax.rsqrt(jnp.sum(jnp.square(t), -1, keepdims=True) + RMS_EPS)


def _heads(t, n):
    return t.reshape(*t.shape[:-1], n, t.shape[-1] // n)


def _pad_cols(w, width):
    return jnp.pad(w, ((0, 0), (0, width - w.shape[1])))


def _pad_rows(w, height):
    return jnp.pad(w, ((0, height - w.shape[0]), (0, 0)))


def _mod_vectors(c, c_ctx, w_mod, b_mod):
    bn, d = c.shape
    cc = jnp.concatenate([c, c_ctx[None], jnp.zeros((8 - bn - 1, d), F32)], 0)
    mv = matmul([(jax.nn.silu(cc), w_mod.astype(BF16))]) + b_mod
    mv = mv.reshape(8, 6, d)
    lat = mv[:bn]
    ctx = jnp.broadcast_to(mv[bn][None], lat.shape)
    return jnp.stack([ctx, lat], 1)


def _pack_mv(gate, shift, scale):
    z = jnp.zeros_like(gate)
    return jnp.stack([gate, shift, scale, z, z, z, z, z], 2)


def _cat_pad(parts):
    return jnp.concatenate([_pad_cols(a, wd) if a.ndim == 2 else jnp.pad(a, (0, wd - a.shape[0]))
                            for a, wd in parts], -1)


def _seg_allsum(x, seg):
    parts = []
    for c0 in range(0, x.shape[-1], LANES):
        t = x[:, c0:c0 + LANES]
        if seg == LANES:
            t = jnp.broadcast_to(jnp.sum(t, -1, keepdims=True), t.shape)
        else:
            lane = lax.broadcasted_iota(jnp.int32, t.shape, 1)
            step = 1
            while step < seg:
                t = t + jnp.where((lane & step) == 0, pltpu.roll(t, LANES - step, 1), pltpu.roll(t, step, 1))
                step *= 2
        parts.append(t)
    return parts[0] if len(parts) == 1 else jnp.concatenate(parts, -1)


def _halo_rows(x, xp_ref, xn_ref, i, n_ctx_tiles):
    has_prev = (i > 0) & (i != n_ctx_tiles)
    has_next = (i != n_ctx_tiles - 1) & (i != pl.num_programs(1) - 1)
    xp = jnp.where(has_prev, xp_ref[0], 0.0)
    xn = jnp.where(has_next, xn_ref[0], 0.0)
    return jnp.concatenate([xp, x, xn], 0)


def _halo_specs(tr, width, l_len, col_block):
    nh, n8 = tr // 8, l_len // 8
    return [pl.BlockSpec((1, tr, width), lambda b, i, *s: (b, i, col_block(*s))),
            pl.BlockSpec((1, 8, width), lambda b, i, *s: (b, jnp.maximum(i * nh - 1, 0), col_block(*s))),
            pl.BlockSpec((1, 8, width), lambda b, i, *s: (b, jnp.minimum((i + 1) * nh, n8 - 1), col_block(*s)))]


def _conv_kernel(x_ref, xp_ref, xn_ref, w_ref, b_ref, o_ref, *, n_ctx_tiles, kw, l2_segments):
    i, s = pl.program_id(1), pl.program_id(2)
    x = x_ref[0]
    tr = x.shape[0]
    xe = _halo_rows(x, xp_ref, xn_ref, i, n_ctx_tiles)
    pad = kw // 2
    acc = jnp.broadcast_to(b_ref[...], x.shape)
    for j in range(kw):
        acc = acc + w_ref[j:j + 1, :] * xe[8 - pad + j:8 - pad + j + tr]
    y = acc * jax.nn.sigmoid(acc)
    if l2_segments:
        yn = y * lax.rsqrt(_seg_allsum(y * y, DN_HEAD_DIM) + RMS_EPS)
        yn = yn * jnp.where(s == 0, DN_HEAD_DIM ** -0.5, 1.0)
        y = jnp.where(s < 2, yn, y)
    o_ref[0] = y


def conv_silu(p, col0, width, seg_w, conv_w, conv_b, n_ctx, l2_segments=False):
    bn, l_len, _ = p.shape
    tr = 256
    cb = col0 // seg_w
    return pl.pallas_call(
        functools.partial(_conv_kernel, n_ctx_tiles=n_ctx // tr, kw=conv_w.shape[0], l2_segments=l2_segments),
        out_shape=jax.ShapeDtypeStruct((bn, l_len, width), F32),
        grid=(bn, l_len // tr, width // seg_w),
        in_specs=_halo_specs(tr, seg_w, l_len, lambda s: cb + s)
        + [pl.BlockSpec((8, seg_w), lambda b, i, s: (0, s)), pl.BlockSpec((1, seg_w), lambda b, i, s: (0, s))],
        out_specs=pl.BlockSpec((1, tr, seg_w), lambda b, i, s: (b, i, s)),
        compiler_params=pltpu.CompilerParams(dimension_semantics=("parallel", "parallel", "parallel"),
                                             vmem_limit_bytes=VMEM_LIMIT),
        name="conv_silu",
    )(p, p, p, _pad_rows(conv_w.astype(F32), 8), conv_b.astype(F32).reshape(1, width))


def _softplus(x):
    return jnp.maximum(x, 0.0) + jnp.log1p(jnp.exp(-jnp.abs(x)))


def _rwkv_prep_kernel(x_ref, xp_ref, xn_ref, mu_ref, vec_ref, a2_ref, g2_ref, w2f_ref, w2b_ref,
                      r_ref, k_ref, v_ref, kk_ref, b_ref, lw_ref, g_ref, *, n_ctx_tiles):
    x = x_ref[0]
    tr = x.shape[0]
    xe = _halo_rows(x, xp_ref, xn_ref, pl.program_id(1), n_ctx_tiles)
    x = x + mu_ref[...] * (0.5 * (xe[7:7 + tr] + xe[9:9 + tr]) - x)
    w = RW_W
    r, k, v = x[:, :w], x[:, w:2 * w], x[:, 2 * w:3 * w]
    o = 3 * w
    wl_f, wl_b, al = x[:, o:o + LANES], x[:, o + LANES:o + 2 * LANES], x[:, o + 2 * LANES:o + 3 * LANES]
    gl = x[:, o + 3 * LANES:o + 3 * LANES + RW_GATE_LORA]
    vec = vec_ref[...]
    a = jax.nn.sigmoid(vec[0:1] + _dot(al, a2_ref[...]))
    g_ref[0] = _dot(jax.nn.sigmoid(gl), g2_ref[...])
    kk = k * vec[1:2]
    kk = kk * lax.rsqrt(_seg_allsum(kk * kk, RW_HEAD_DIM) + RMS_EPS)
    r_ref[0] = r
    v_ref[0] = v
    k_ref[0] = k * (1.0 + (a - 1.0) * vec[2:3])
    kk_ref[0] = kk
    b_ref[0] = kk * a
    for di, (wl, w2_ref) in enumerate(((wl_f, w2f_ref), (wl_b, w2b_ref))):
        w_log = -_softplus(-(vec[3 + di:4 + di] + _dot(jnp.tanh(wl), w2_ref[...]))) - 0.5
        lw_ref[di, 0] = -jnp.exp(w_log)


def rwkv_prep(p_rw, mu, vec, a2, g2, w2f, w2b, n_ctx):
    bn, l_len, wp = p_rw.shape
    tr = 128
    w = RW_W
    full = lambda a: pl.BlockSpec(a.shape, lambda b, i: (0,) * a.ndim)
    seq = pl.BlockSpec((1, tr, w), lambda b, i: (b, i, 0))
    sds = jax.ShapeDtypeStruct((bn, l_len, w), F32)
    return pl.pallas_call(
        functools.partial(_rwkv_prep_kernel, n_ctx_tiles=n_ctx // tr),
        out_shape=[sds, sds, sds, sds, sds, jax.ShapeDtypeStruct((2, bn, l_len, w), F32), sds],
        grid=(bn, l_len // tr),
        in_specs=_halo_specs(tr, wp, l_len, lambda: 0) + [full(mu), full(vec), full(a2), full(g2), full(w2f), full(w2b)],
        out_specs=[seq, seq, seq, seq, seq, pl.BlockSpec((2, 1, tr, w), lambda b, i: (0, b, i, 0)), seq],
        compiler_params=pltpu.CompilerParams(dimension_semantics=("parallel", "parallel"),
                                             vmem_limit_bytes=VMEM_LIMIT),
        name="rwkv_prep",
    )(p_rw, p_rw, p_rw, mu, vec, a2, g2, w2f, w2b)


def _l0_out_kernel(att_ref, of_ref, ob_ref, zg_ref, nw_ref, w_ref, z_ref, mv_ref, g_ref, b_ref, zn_ref, h_ref):
    o = of_ref[0] + ob_ref[0]
    o = o * lax.rsqrt(_seg_allsum(o * o, DN_HEAD_DIM) * (1.0 / DN_HEAD_DIM) + RMS_EPS) * nw_ref[...]
    zg = zg_ref[0]
    dn = o * (zg * jax.nn.sigmoid(zg))
    mix = _dot(att_ref[0], w_ref[:A_Q]) + _dot(dn, w_ref[A_Q:])
    _residual_ln_mod(z_ref[0], mix, mv_ref[0, 0], g_ref[...], b_ref[...], zn_ref, h_ref)


def _l1_out_kernel(yf_ref, yb_ref, xs_ref, zs_ref, rf_ref, rb_ref, r_ref, k_ref, v_ref, gt_ref, vec_ref, w_ref,
                   z_ref, mv_ref, g_ref, b_ref, zn_ref, h_ref):
    vec = vec_ref[...]
    zs = zs_ref[0]
    y = (yf_ref[0, 0] + yb_ref[0, 0] + vec[0:1] * xs_ref[0]) * (zs * jax.nn.sigmoid(zs))
    gw = SSD_W // SSD_GROUPS
    parts = []
    for gi in range(SSD_GROUPS):
        yg = y[:, gi * gw:(gi + 1) * gw]
        parts.append(yg * lax.rsqrt(jnp.mean(yg * yg, -1, keepdims=True) + RMS_EPS))
    ssd = jnp.concatenate(parts, -1) * vec[1:2]
    yr = rf_ref[0, 0] + rb_ref[0, 0]
    inv = 1.0 / RW_HEAD_DIM
    yc = yr - _seg_allsum(yr, RW_HEAD_DIM) * inv
    var = _seg_allsum(yc * yc, RW_HEAD_DIM) * inv
    yr = yc * lax.rsqrt(var + RW_LN_EPS) * vec[2:3] + vec[3:4]
    yr = yr + _seg_allsum(r_ref[0] * k_ref[0] * vec[4:5], RW_HEAD_DIM) * v_ref[0]
    rw = yr * gt_ref[0]
    mix = _dot(ssd, w_ref[:SSD_W]) + _dot(rw, w_ref[SSD_W:])
    _residual_ln_mod(z_ref[0], mix, mv_ref[0, 0], g_ref[...], b_ref[...], zn_ref, h_ref)


def _mixer_out_call(body, name, tr, seq_inputs, vec_inputs, w_out, z, mv, ln_g, ln_b, n_ctx):
    bn, l_len, d = z.shape
    nbc = n_ctx // tr
    full = lambda a: pl.BlockSpec(a.shape, lambda b, i: (0,) * a.ndim)
    row = pl.BlockSpec((1, tr, d), lambda b, i: (b, i, 0))
    w_b = w_out.astype(BF16)
    return pl.pallas_call(
        body,
        out_shape=[jax.ShapeDtypeStruct((bn, l_len, d), F32), jax.ShapeDtypeStruct((bn, l_len, d), BF16)],
        grid=(bn, l_len // tr),
        in_specs=[pl.BlockSpec(bs, im) for _, bs, im in seq_inputs] + [full(a) for a in vec_inputs]
        + [pl.BlockSpec(w_b.shape, lambda b, i: (0, 0), pipeline_mode=pl.Buffered(1)), row,
           pl.BlockSpec((1, 1, 8, d), lambda b, i: (b, jnp.where(i < nbc, 0, 1), 0, 0)),
           pl.BlockSpec((1, d), lambda b, i: (0, 0)), pl.BlockSpec((1, d), lambda b, i: (0, 0))],
        out_specs=[row, row],
        compiler_params=pltpu.CompilerParams(dimension_semantics=("parallel", "parallel"),
                                             vmem_limit_bytes=VMEM_LIMIT),
        name=name,
    )(*[a for a, _, _ in seq_inputs], *vec_inputs, w_b, z, mv, ln_g.reshape(1, d), ln_b.reshape(1, d))


def _layer0(z, h, mv, n_ctx, w_in, w_out, attn_sink, conv_w, a_log, dt_bias, norm_w, ln_g, ln_b, cs):
    bn, l_len, d = h.shape
    q, k, v, dq, dk, dv, zz, sm = jnp.split(
        w_in, np.cumsum([A_Q, A_KV, A_KV, DN_W, DN_W, DN_W, DN_W]).tolist(), 1)
    w_in_p = jnp.concatenate([q, dq, dk, dv, zz, k, v, _pad_cols(sm, LANES)], 1).astype(BF16)
    k_col = A_Q + 4 * DN_W
    p = matmul([(h.reshape(bn * l_len, d), w_in_p)]).reshape(bn, l_len, -1)
    att = attention(p, cs, attn_sink, n_ctx, 0, k_col, k_col + A_KV)
    qkv = conv_silu(p, A_Q, 3 * DN_W, DN_W, conv_w, jnp.zeros((3 * DN_W,), F32), n_ctx, l2_segments=True)
    sm_col = k_col + 2 * A_KV
    sm = p[..., sm_col:sm_col + 3 * DN_HEADS]
    a_f, a_b, bb = sm[..., :DN_HEADS], sm[..., DN_HEADS:2 * DN_HEADS], sm[..., 2 * DN_HEADS:]
    beta = jax.nn.sigmoid(bb)
    g = jnp.stack([-jnp.exp(a_log[0]) * jax.nn.softplus(a_f + dt_bias[0]),
                   -jnp.exp(a_log[1]) * jax.nn.softplus(a_b + dt_bias[1])], 0)
    o = dn_scan(qkv, qkv, qkv, beta, g, n_ctx, col_blocks=(0, 1, 2))
    tr = 256
    seq = [(att, (1, tr, A_Q), lambda b, i: (b, i, 0)),
           (o[0], (1, tr, DN_W), lambda b, i: (b, i, 0)),
           (o[1], (1, tr, DN_W), lambda b, i: (b, i, 0)),
           (p, (1, tr, DN_W), lambda b, i: (b, i, (A_Q + 3 * DN_W) // DN_W))]
    return _mixer_out_call(_l0_out_kernel, "l0_out", tr, seq, [jnp.tile(norm_w, DN_HEADS).reshape(1, DN_W)],
                           w_out, z, mv, ln_g, ln_b, n_ctx)


def _layer1(z, h, mv, n_ctx, w_in, w_out, ssd_conv_w, ssd_conv_b, ssd_a_log, ssd_dt_bias, ssd_d, ssd_norm_w,
            rw_mu, rw_w0, rw_w2, rw_a0, rw_a2, rw_g2, rw_k_k, rw_k_a, rw_r_k, rw_ln_g, rw_ln_b, ln_g, ln_b):
    bn, l_len, d = h.shape
    n_tok = bn * l_len
    ssd_in = 2 * SSD_W + 2 * SSD_GN + 2 * SSD_HEADS
    lora_w = [RW_DECAY_LORA, RW_DECAY_LORA, RW_A_LORA]
    h2 = h.reshape(n_tok, d)
    w_ssd = _pad_cols(w_in[:, :ssd_in], 5120).astype(BF16)
    p_ssd = matmul([(h2, w_ssd)]).reshape(bn, l_len, -1)
    xbc = conv_silu(p_ssd, SSD_W, SSD_W + 2 * SSD_GN, 512, ssd_conv_w, ssd_conv_b, n_ctx)
    dtc = p_ssd[..., 2 * SSD_W + 2 * SSD_GN:ssd_in]
    dt = jnp.stack([jax.nn.softplus(dtc[..., :SSD_HEADS] + ssd_dt_bias[0]),
                    jax.nn.softplus(dtc[..., SSD_HEADS:] + ssd_dt_bias[1])], 0)
    da = -jnp.exp(ssd_a_log)[:, None, None, :] * dt
    y_ssd = ssd_scan(xbc, xbc, xbc, dt, da, n_ctx,
                     col_blocks=((SSD_W + SSD_GN) // SSD_GN, SSD_W // SSD_GN, 0))
    splits = np.cumsum([RW_W, RW_W, RW_W] + lora_w).tolist()
    widths = [RW_W, RW_W, RW_W, LANES, LANES, LANES, RW_GATE_LORA]
    w_rw = _cat_pad(list(zip(jnp.split(w_in[:, ssd_in:], splits, 1), widths)))
    w_rw = _pad_cols(w_rw, 6912).astype(BF16)
    mu = _cat_pad(list(zip(jnp.split(rw_mu, splits), widths)))
    mu = jnp.pad(mu, (0, w_rw.shape[1] - mu.shape[0])).reshape(1, -1)
    p_rw = matmul([(h2, w_rw)]).reshape(bn, l_len, -1)
    zero = jnp.zeros((RW_W,), F32)
    vec = jnp.stack([rw_a0, rw_k_k, rw_k_a, rw_w0[0], rw_w0[1], zero, zero, zero], 0)
    lora_p = lambda w: _pad_rows(w, LANES).astype(BF16)
    r, k, v, kk, b, lw, g = rwkv_prep(p_rw, mu, vec, lora_p(rw_a2), rw_g2.astype(BF16), lora_p(rw_w2[0]),
                                      lora_p(rw_w2[1]), n_ctx)
    y_rw = rwkv_scan(r, k, v, kk, b, lw, n_ctx)
    tr = 128
    rowmap = lambda b_, i: (b_, i, 0)
    seq = [(y_ssd, (1, 1, tr, SSD_W), lambda b_, i: (0, b_, i, 0)),
           (y_ssd, (1, 1, tr, SSD_W), lambda b_, i: (1, b_, i, 0)),
           (xbc, (1, tr, SSD_W), rowmap), (p_ssd, (1, tr, SSD_W), rowmap),
           (y_rw, (1, 1, tr, RW_W), lambda b_, i: (0, b_, i, 0)),
           (y_rw, (1, 1, tr, RW_W), lambda b_, i: (1, b_, i, 0)),
           (r, (1, tr, RW_W), rowmap), (k, (1, tr, RW_W), rowmap), (v, (1, tr, RW_W), rowmap),
           (g, (1, tr, RW_W), rowmap)]
    vec_out = jnp.stack([jnp.repeat(ssd_d, SSD_HEAD_DIM), ssd_norm_w, rw_ln_g, rw_ln_b, rw_r_k.reshape(-1),
                         zero, zero, zero], 0)
    return _mixer_out_call(_l1_out_kernel, "l1_out", tr, seq, [vec_out], w_out, z, mv, ln_g, ln_b, n_ctx)


def kernel(x, c, ctx, c_ctx, l0_w_mod, l0_b_mod, l0_ln1_g, l0_ln1_b, l0_ln2_g, l0_ln2_b, l0_w_in, l0_w_out, l0_attn_sink, l0_dn_conv_w, l0_dn_a_log, l0_dn_dt_bias, l0_dn_norm_w, l0_w_router, l0_router_bias, l0_we_gate, l0_we_up, l0_we_down, l0_ws_gate, l0_ws_up, l0_ws_down, l1_w_mod, l1_b_mod, l1_ln1_g, l1_ln1_b, l1_ln2_g, l1_ln2_b, l1_w_in, l1_w_out, l1_ssd_conv_w, l1_ssd_conv_b, l1_ssd_a_log, l1_ssd_dt_bias, l1_ssd_d, l1_ssd_norm_w, l1_rw_mu, l1_rw_w0, l1_rw_w2, l1_rw_a0, l1_rw_a2, l1_rw_g2, l1_rw_k_k, l1_rw_k_a, l1_rw_r_k, l1_rw_ln_g, l1_rw_ln_b, l1_w_router, l1_router_bias, l1_we_gate, l1_we_up, l1_we_down, l1_ws_gate, l1_ws_up, l1_ws_down):
    bn, s_len, d = x.shape
    n_ctx = ctx.shape[1]
    l_len = n_ctx + s_len
    n_tok = bn * l_len
    z = jnp.concatenate([ctx, x], 1)
    cs = rope_table(n_ctx, s_len)
    mv0 = _mod_vectors(c, c_ctx, l0_w_mod, l0_b_mod)
    mv1 = _mod_vectors(c, c_ctx, l1_w_mod, l1_b_mod)
    pick = lambda mv, gi, shi, sci: _pack_mv(mv[:, :, gi], mv[:, :, shi], mv[:, :, sci])

    (h,) = ln_mod(z, z, pick(mv0, 2, 0, 1), l0_ln1_g, l0_ln1_b, n_ctx, do_ln=False, do_mod=True)
    z, h = _layer0(z, h, pick(mv0, 2, 3, 4), n_ctx, l0_w_in, l0_w_out, l0_attn_sink, l0_dn_conv_w, l0_dn_a_log,
                   l0_dn_dt_bias, l0_dn_norm_w, l0_ln1_g, l0_ln1_b, cs)
    f = moe(h.reshape(n_tok, d), l0_w_router, l0_router_bias, l0_we_gate, l0_we_up, l0_we_down,
            l0_ws_gate, l0_ws_up, l0_ws_down).reshape(bn, l_len, d)
    mv_a = _pack_mv(mv0[:, :, 5], mv1[:, :, 0], mv1[:, :, 1])
    z, h = ln_mod(z, f, mv_a, l0_ln2_g, l0_ln2_b, n_ctx, do_ln=True, do_mod=True)

    z, h = _layer1(z, h, pick(mv1, 2, 3, 4), n_ctx, l1_w_in, l1_w_out, l1_ssd_conv_w, l1_ssd_conv_b, l1_ssd_a_log,
                   l1_ssd_dt_bias, l1_ssd_d, l1_ssd_norm_w, l1_rw_mu, l1_rw_w0, l1_rw_w2, l1_rw_a0, l1_rw_a2,
                   l1_rw_g2, l1_rw_k_k, l1_rw_k_a, l1_rw_r_k, l1_rw_ln_g, l1_rw_ln_b, l1_ln1_g, l1_ln1_b)
    f = moe(h.reshape(n_tok, d), l1_w_router, l1_router_bias, l1_we_gate, l1_we_up, l1_we_down,
            l1_ws_gate, l1_ws_up, l1_ws_down).reshape(bn, l_len, d)
    (zx,) = ln_mod(z, f, pick(mv1, 5, 3, 4), l1_ln2_g, l1_ln2_b, n_ctx, do_ln=True, do_mod=False, lat_only=True)
    return zx
```

```python
import functools

import jax
import jax.numpy as jnp
import numpy as np
from jax import lax
from jax.experimental import pallas as pl
from jax.experimental.pallas import tpu as pltpu

F32 = jnp.float32
BF16 = jnp.bfloat16

DEPTH = 2
GRID_W = 64
HEAD_DIM = 128
A_Q_HEADS = 8
A_KV_HEADS = 2
WINDOW = 128
ATT_BLOCK = 128
ROPE_THETA = 10000.0
DN_HEADS = 8
DN_HEAD_DIM = 128
SSD_HEADS = 32
SSD_HEAD_DIM = 64
SSD_GROUPS = 2
SSD_STATE = 128
RW_HEADS = 32
RW_HEAD_DIM = 64
RW_DECAY_LORA = 96
RW_A_LORA = 96
RW_GATE_LORA = 256
RW_LN_EPS = 64e-5
N_EXPERTS = 64
TOP_K = 8
N_GROUPS = 8
TOPK_GROUPS = 4
ROUTED_SCALE = 2.5
LN_EPS = 1e-5
RMS_EPS = 1e-6
DEEPNORM_ALPHA = (2 * DEPTH) ** 0.25

A_Q = A_Q_HEADS * HEAD_DIM
A_KV = A_KV_HEADS * HEAD_DIM
DN_W = DN_HEADS * DN_HEAD_DIM
SSD_W = SSD_HEADS * SSD_HEAD_DIM
SSD_GN = SSD_GROUPS * SSD_STATE
RW_W = RW_HEADS * RW_HEAD_DIM

LANES = 128
SCAN_CHUNK = 64
VMEM_LIMIT = 56 * 1024 * 1024
NEG_BIG = -1e30

HIGHEST = lax.Precision.HIGHEST


def _dot_dims(a, b, dims, precision):
    if precision is None:
        a, b = a.astype(BF16), b.astype(BF16)
    return lax.dot_general(a, b, (dims, ((), ())), preferred_element_type=F32, precision=precision)


def _dot(a, b, precision=None):
    return _dot_dims(a, b, ((1,), (0,)), precision)


def _dot_nt(a, b, precision=None):
    return _dot_dims(a, b, ((1,), (1,)), precision)


def _dot_tn(a, b, precision=None):
    return _dot_dims(a, b, ((0,), (0,)), precision)


def _split3(x):
    hi = x.astype(BF16)
    r1 = x - hi.astype(F32)
    mid = r1.astype(BF16)
    return hi, mid, (r1 - mid.astype(F32)).astype(BF16)


def _dot_sel(x, sel01):
    s = sel01.astype(BF16)
    hi, mid, lo = _split3(x)
    return _dot(hi, s) + _dot(mid, s) + _dot(lo, s)


def _sel_dot(sel01, x):
    s = sel01.astype(BF16)
    hi, mid, lo = _split3(x)
    return _dot(s, hi) + _dot(s, mid) + _dot(s, lo)


def _pick_tile(n, candidates):
    for c in candidates:
        if n % c == 0:
            return c
    return n


def _mm_kernel(*refs, n_pairs):
    o_ref = refs[2 * n_pairs]
    acc = None
    for a_ref, w_ref in zip(refs[:n_pairs], refs[n_pairs:2 * n_pairs]):
        p = _dot(a_ref[...].astype(BF16), w_ref[...].astype(BF16))
        acc = p if acc is None else acc + p
    o_ref[...] = acc.astype(o_ref.dtype)


def matmul(pairs, out_dtype=F32):
    m = pairs[0][0].shape[0]
    n = pairs[0][1].shape[1]
    tn = _pick_tile(n, (1280, 1152, 1024, 768, 512, 384, 256, 128))
    tm = _pick_tile(m, (1024, 512, 256, 128) if tn <= 768 else (512, 256, 128))
    in_specs = ([pl.BlockSpec((tm, a.shape[1]), lambda j, i: (i, 0)) for a, _ in pairs]
                + [pl.BlockSpec((w.shape[0], tn), lambda j, i: (0, j)) for _, w in pairs])
    return pl.pallas_call(
        functools.partial(_mm_kernel, n_pairs=len(pairs)),
        out_shape=jax.ShapeDtypeStruct((m, n), out_dtype),
        grid=(n // tn, m // tm),
        in_specs=in_specs,
        out_specs=pl.BlockSpec((tm, tn), lambda j, i: (i, j)),
        compiler_params=pltpu.CompilerParams(dimension_semantics=("parallel", "parallel"),
                                             vmem_limit_bytes=VMEM_LIMIT),
        name="matmul",
    )(*[a for a, _ in pairs], *[w for _, w in pairs])


def _residual_ln_mod(z, o, mv, g, b, zn_ref, h_ref):
    t = DEEPNORM_ALPHA * z + mv[0:1] * o
    tc = t - jnp.mean(t, -1, keepdims=True)
    zn = tc * lax.rsqrt(jnp.mean(tc * tc, -1, keepdims=True) + LN_EPS) * g + b
    zn_ref[0] = zn
    h_ref[0] = (zn * (1.0 + mv[2:3]) + mv[1:2]).astype(h_ref.dtype)


def _ln_mod_kernel(z_ref, o_ref, mv_ref, g_ref, b_ref, zn_ref, h_ref, *, do_ln, do_mod):
    z = z_ref[0]
    mv = mv_ref[0, 0]
    if do_ln:
        t = DEEPNORM_ALPHA * z + mv[0:1] * o_ref[0]
        mu = jnp.mean(t, -1, keepdims=True)
        tc = t - mu
        var = jnp.mean(tc * tc, -1, keepdims=True)
        z = tc * lax.rsqrt(var + LN_EPS) * g_ref[...] + b_ref[...]
        zn_ref[0] = z
    if do_mod:
        h_ref[0] = (z * (1.0 + mv[2:3]) + mv[1:2]).astype(h_ref.dtype)


def ln_mod(z, o, mv, ln_g, ln_b, n_ctx, *, do_ln, do_mod, lat_only=False):
    bn, l_len, d = z.shape
    tr = 256
    nbc = n_ctx // tr
    off = nbc if lat_only else 0
    n_out = l_len - off * tr
    row = lambda b, i: (b, i + off, 0)
    outs, out_specs = [], []
    if do_ln:
        outs.append(jax.ShapeDtypeStruct((bn, n_out, d), F32))
        out_specs.append(pl.BlockSpec((1, tr, d), lambda b, i: (b, i, 0)))
    if do_mod:
        outs.append(jax.ShapeDtypeStruct((bn, n_out, d), BF16))
        out_specs.append(pl.BlockSpec((1, tr, d), lambda b, i: (b, i, 0)))

    def body(z_ref, o_ref, mv_ref, g_ref, b_ref, *out_refs):
        zn_ref = out_refs[0] if do_ln else None
        h_ref = out_refs[-1] if do_mod else None
        _ln_mod_kernel(z_ref, o_ref, mv_ref, g_ref, b_ref, zn_ref, h_ref, do_ln=do_ln, do_mod=do_mod)

    res = pl.pallas_call(
        body,
        out_shape=outs,
        grid=(bn, n_out // tr),
        in_specs=[pl.BlockSpec((1, tr, d), row), pl.BlockSpec((1, tr, d), row),
                  pl.BlockSpec((1, 1, 8, d), lambda b, i: (b, jnp.where(i + off < nbc, 0, 1), 0, 0)),
                  pl.BlockSpec((1, d), lambda b, i: (0, 0)), pl.BlockSpec((1, d), lambda b, i: (0, 0))],
        out_specs=out_specs,
        compiler_params=pltpu.CompilerParams(dimension_semantics=("parallel", "parallel"),
                                             vmem_limit_bytes=VMEM_LIMIT),
        name="ln_mod",
    )(z, o, mv, ln_g.reshape(1, d), ln_b.reshape(1, d))
    return res


def _rope(x, cs):
    lane = lax.broadcasted_iota(jnp.int32, x.shape, 1)
    quarter = HEAD_DIM // 4
    first = (lane % (2 * quarter)) < quarter
    partner = jnp.where(first, pltpu.roll(x, HEAD_DIM - quarter, 1), pltpu.roll(x, quarter, 1))
    return x * cs[:, :HEAD_DIM] + partner * cs[:, HEAD_DIM:]


def _attn_kernel(q_ref, kp_ref, kc_ref, kn_ref, vp_ref, vc_ref, vn_ref, kx_ref, vx_ref,
                 csp_ref, csc_ref, csn_ref, sink_ref, o_ref, *, n_ctx_blocks, s_len):
    i = pl.program_id(1)
    blk = ATT_BLOCK
    grp = A_Q_HEADS // A_KV_HEADS
    n_ctx = kx_ref.shape[1]
    csc = csc_ref[...]
    q = q_ref[0]
    q4 = jnp.concatenate([_rope(q[:, h * HEAD_DIM:(h + 1) * HEAD_DIM], csc) for h in range(grp)], 0)
    k_all = jnp.concatenate([_rope(kp_ref[0], csp_ref[...]), _rope(kc_ref[0], csc),
                             _rope(kn_ref[0], csn_ref[...]), kx_ref[0]], 0)
    v_all = jnp.concatenate([vp_ref[0], vc_ref[0], vn_ref[0], vx_ref[0]], 0)
    n_keys = 3 * blk + n_ctx
    s = _dot_nt(q4, k_all) * HEAD_DIM ** -0.5
    s = s.reshape(grp, blk, n_keys)
    tq = lax.broadcasted_iota(jnp.int32, (1, blk, n_keys), 1)
    col = lax.broadcasted_iota(jnp.int32, (1, blk, n_keys), 2)
    koff = col - blk
    pos = (i - n_ctx_blocks) * blk + koff
    local_ok = (jnp.abs(tq - koff) <= WINDOW) & (pos >= 0) & (pos < s_len) & (i >= n_ctx_blocks)
    valid = local_ok | (col >= 3 * blk)
    s = jnp.where(valid, s, -jnp.inf)
    sink = sink_ref[0]
    hsel = lax.broadcasted_iota(jnp.int32, (grp, 1, 1), 0)
    sink3 = jnp.zeros((grp, 1, 1), F32)
    for h in range(grp):
        sink3 = jnp.where(hsel == h, sink[:, h:h + 1].reshape(1, 1, 1), sink3)
    m = jnp.maximum(jnp.max(s, -1, keepdims=True), sink3)
    p = jnp.exp(s - m)
    denom = jnp.sum(p, -1, keepdims=True) + jnp.exp(sink3 - m)
    o = _dot(p.reshape(grp * blk, n_keys), v_all).reshape(grp, blk, HEAD_DIM) / denom
    for h in range(grp):
        o_ref[0, :, h * HEAD_DIM:(h + 1) * HEAD_DIM] = o[h]


def attention(p_all, cs, sink, n_ctx, q_col, k_col, v_col):
    bn, l_len, _ = p_all.shape
    blk = ATT_BLOCK
    grp = A_Q_HEADS // A_KV_HEADS
    nb = l_len // blk
    ncb = n_ctx // blk
    s_len = l_len - n_ctx
    qw = grp * HEAD_DIM
    prev = lambda i: jnp.clip(i - 1, ncb, nb - 1)
    cur = lambda i: jnp.clip(i, ncb, nb - 1)
    nxt = lambda i: jnp.clip(i + 1, ncb, nb - 1)
    kb, vb = k_col // HEAD_DIM, v_col // HEAD_DIM

    def kv_spec(colb, rowf):
        return pl.BlockSpec((1, blk, HEAD_DIM), lambda b, i, h: (b, rowf(i), colb + h))

    def cs_spec(rowf):
        return pl.BlockSpec((blk, 2 * HEAD_DIM), lambda b, i, h: (rowf(i), 0))

    return pl.pallas_call(
        functools.partial(_attn_kernel, n_ctx_blocks=ncb, s_len=s_len),
        out_shape=jax.ShapeDtypeStruct((bn, l_len, A_Q), F32),
        grid=(bn, nb, A_KV_HEADS),
        in_specs=[pl.BlockSpec((1, blk, qw), lambda b, i, h: (b, i, q_col // qw + h)),
                  kv_spec(kb, prev), kv_spec(kb, cur), kv_spec(kb, nxt),
                  kv_spec(vb, prev), kv_spec(vb, cur), kv_spec(vb, nxt),
                  pl.BlockSpec((1, n_ctx, HEAD_DIM), lambda b, i, h: (b, 0, kb + h)),
                  pl.BlockSpec((1, n_ctx, HEAD_DIM), lambda b, i, h: (b, 0, vb + h)),
                  cs_spec(prev), pl.BlockSpec((blk, 2 * HEAD_DIM), lambda b, i, h: (i, 0)), cs_spec(nxt),
                  pl.BlockSpec((1, 1, grp), lambda b, i, h: (h, 0, 0))],
        out_specs=pl.BlockSpec((1, blk, qw), lambda b, i, h: (b, i, h)),
        compiler_params=pltpu.CompilerParams(dimension_semantics=("parallel", "parallel", "parallel"),
                                             vmem_limit_bytes=VMEM_LIMIT),
        name="window_attention",
    )(p_all, p_all, p_all, p_all, p_all, p_all, p_all, p_all, p_all, cs, cs, cs,
      sink.astype(F32).reshape(A_KV_HEADS, 1, grp))


def rope_table(n_ctx, s_len):
    rows = s_len // GRID_W
    row = jnp.repeat(jnp.arange(rows, dtype=F32), GRID_W)
    colp = jnp.tile(jnp.arange(GRID_W, dtype=F32), rows)
    n_freq = HEAD_DIM // 4
    inv_freq = ROPE_THETA ** (-jnp.arange(n_freq, dtype=F32) / n_freq)
    ang_r = row[:, None] * inv_freq
    ang_c = colp[:, None] * inv_freq
    cos = jnp.concatenate([jnp.cos(ang_r), jnp.cos(ang_r), jnp.cos(ang_c), jnp.cos(ang_c)], -1)
    sin = jnp.concatenate([-jnp.sin(ang_r), jnp.sin(ang_r), -jnp.sin(ang_c), jnp.sin(ang_c)], -1)
    lat = jnp.concatenate([cos, sin], -1)
    ctx = jnp.concatenate([jnp.ones((n_ctx, HEAD_DIM), F32), jnp.zeros((n_ctx, HEAD_DIM), F32)], -1)
    return jnp.concatenate([ctx, lat], 0)


def _order_masks(c, d):
    t = lax.broadcasted_iota(jnp.int32, (c, c), 0)
    j = lax.broadcasted_iota(jnp.int32, (c, c), 1)
    rel = (t - j) * (1 - 2 * d)
    return rel >= 0, rel > 0


def _unit_inverses(ms):
    c = ms[0].shape[0]
    eye = (lax.broadcasted_iota(jnp.int32, (c, c), 0) == lax.broadcasted_iota(jnp.int32, (c, c), 1)).astype(F32)
    ts = [eye + m for m in ms]
    pws = [_dot(m, m) for m in ms]
    n = 2
    while 2 * n < c:
        st = [_dot(jnp.concatenate([t, p], 0), p) for t, p in zip(ts, pws)]
        ts = [t + s[:c] for t, s in zip(ts, st)]
        pws = [s[c:] for s in st]
        n *= 2
    return [t + _dot(t, p) for t, p in zip(ts, pws)]


def _scan_chunk(hts, r_t, a_t, v, k_h, b_h, pc, n_rk, n_rb=None, m_ab=None, m_ak=None):
    zipm = lambda f, *ls: [f(*xs) for xs in zip(*ls)]
    if m_ab is None:
        y0 = zipm(_dot, n_rk, v)
        s0 = zipm(_dot_tn, v, k_h)
        ys = zipm(lambda r, ht, y: _dot_nt(r, ht) + y, r_t, hts, y0)
        return ys, zipm(lambda ht, p, s: ht * p + s, hts, pc, s0)
    dk = a_t[0].shape[1]
    ts = _unit_inverses(m_ab)
    mv = zipm(_dot, m_ak, v)
    wu = zipm(lambda t, a, m: _dot(t, jnp.concatenate([a, m], 1)), ts, a_t, mv)
    nwu = zipm(_dot, n_rb, wu)
    nkv = zipm(_dot, n_rk, v)
    gs = zipm(_dot_tn, wu, b_h)
    vk = zipm(_dot_tn, v, k_h)
    ys = zipm(lambda r, n, ht, y: _dot_nt(r + n[:, :dk], ht) + n[:, dk:] + y, r_t, nwu, hts, nkv)
    return ys, zipm(lambda ht, p, g_, s: ht * p + _dot(ht, g_[:dk]) + g_[dk:] + s, hts, pc, gs, vk)


def _time_block(d, j, ncc, nct):
    bwd = jnp.where(j < ncc, ncc - 1 - j, nct - 1 - j + ncc)
    return jnp.where(d == 0, j, bwd)


def _cum_scalars(col_g, row_g, incl):
    tri = incl.astype(F32)
    cum_col = _dot(tri, col_g, HIGHEST)
    cum_row = _dot_nt(row_g, tri, HIGHEST)
    tot = jnp.sum(col_g, 0, keepdims=True)
    return cum_col, cum_row, tot


def _dn_chunk_args(q_ref, k_ref, v_ref, col_ref, row_ref, d, heads, dh):
    c = q_ref.shape[1]
    incl, strict = _order_masks(c, d)
    col = col_ref[0, 0]
    row = row_ref[0, 0, 0]
    cum_col, cum_row, tot = _cum_scalars(col[:, heads:], row[heads:], incl)
    hs = range(heads)
    sls = [slice(h * dh, (h + 1) * dh) for h in hs]
    qs, ks, vs = ([ref[0, :, sl] for sl in sls] for ref in (q_ref, k_ref, v_ref))
    kks = [_dot_nt(k, k) for k in ks]
    qks = [_dot_nt(q, k) for q, k in zip(qs, ks)]
    args = [[] for _ in range(10)]
    for h in hs:
        q, k = qs[h], ks[h]
        bc, gc = col[:, h:h + 1], col[:, heads + h:heads + h + 1]
        br, gr = row[h:h + 1], row[heads + h:heads + h + 1]
        cc, cr, tt = cum_col[:, h:h + 1], cum_row[h:h + 1], tot[:, h:h + 1]
        kkb = kks[h] * br
        qkb = qks[h] * br
        dif = cc - cr
        m_ak = kkb * jnp.exp(jnp.where(strict, dif - gc, NEG_BIG))
        m_ab = -kkb * jnp.exp(jnp.where(strict, dif - gc + gr, NEG_BIG))
        n_rk = qkb * jnp.exp(jnp.where(incl, dif, NEG_BIG))
        n_rb = -qkb * jnp.exp(jnp.where(incl, dif + gr, NEG_BIG))
        a_t = k * jnp.exp(cc - gc)
        r_t = q * jnp.exp(cc)
        k_h = k * (bc * jnp.exp(tt - cc))
        b_h = -k * (bc * jnp.exp(gc + tt - cc))
        for lst, val in zip(args, (r_t, a_t, vs[h], k_h, b_h, jnp.exp(tt), n_rk, n_rb, m_ab, m_ak)):
            lst.append(val)
    return args


def _dn_scan_kernel(qf_ref, kf_ref, vf_ref, colf_ref, rowf_ref, qb_ref, kb_ref, vb_ref, colb_ref, rowb_ref,
                    of_ref, ob_ref, st_ref, *, heads, dh):
    @pl.when(pl.program_id(1) == 0)
    def _():
        st_ref[...] = jnp.zeros_like(st_ref)

    fwd = _dn_chunk_args(qf_ref, kf_ref, vf_ref, colf_ref, rowf_ref, 0, heads, dh)
    bwd = _dn_chunk_args(qb_ref, kb_ref, vb_ref, colb_ref, rowb_ref, 1, heads, dh)
    ys, hts = _scan_chunk([st_ref[h] for h in range(2 * heads)], *[f + b for f, b in zip(fwd, bwd)])
    for h in range(2 * heads):
        st_ref[h] = hts[h]
    for h in range(heads):
        of_ref[0, :, h * dh:(h + 1) * dh] = ys[h]
        ob_ref[0, :, h * dh:(h + 1) * dh] = ys[heads + h]


def dn_scan(q, k, v, beta, g, n_ctx, col_blocks=(0, 0, 0)):
    bn, l_len, _ = q.shape
    heads = beta.shape[-1]
    dh = DN_HEAD_DIM
    w = heads * dh
    c = SCAN_CHUNK
    nct, ncc = l_len // c, n_ctx // c
    col = jnp.concatenate([jnp.broadcast_to(beta[None], g.shape), g], -1)
    row = jnp.swapaxes(col.reshape(2, bn, nct, c, 2 * heads), 3, 4)
    tb = lambda d, j: _time_block(d, j, ncc, nct)
    seq = lambda d, cb: pl.BlockSpec((1, c, w), lambda b, j: (b, tb(d, j), cb))
    side = lambda d: [seq(d, col_blocks[0]), seq(d, col_blocks[1]), seq(d, col_blocks[2]),
                      pl.BlockSpec((1, 1, c, 2 * heads), lambda b, j: (d, b, tb(d, j), 0)),
                      pl.BlockSpec((1, 1, 1, 2 * heads, c), lambda b, j: (d, b, tb(d, j), 0, 0))]
    out = jax.ShapeDtypeStruct((bn, l_len, w), F32)
    return pl.pallas_call(
        functools.partial(_dn_scan_kernel, heads=heads, dh=dh),
        out_shape=[out, out],
        grid=(bn, nct),
        in_specs=side(0) + side(1),
        out_specs=[pl.BlockSpec((1, c, w), lambda b, j: (b, tb(0, j), 0)),
                   pl.BlockSpec((1, c, w), lambda b, j: (b, tb(1, j), 0))],
        scratch_shapes=[pltpu.VMEM((2 * heads, dh, dh), F32)],
        compiler_params=pltpu.CompilerParams(dimension_semantics=("parallel", "arbitrary")),
        name="dn_scan",
    )(q, k, v, col, row, q, k, v, col, row)


def _ssd_scan_kernel(c_ref, b_ref, x_ref, col_ref, row_ref, o_ref, st_ref, *, heads, groups, dh, n_st):
    d = pl.program_id(1)

    @pl.when(pl.program_id(2) == 0)
    def _():
        st_ref[...] = jnp.zeros_like(st_ref)

    c = c_ref.shape[1]
    incl, _ = _order_masks(c, d)
    col = col_ref[0, 0]
    row = row_ref[0, 0, 0]
    cum_col, cum_row, tot = _cum_scalars(col[:, heads:], row, incl)
    hg = heads // groups
    gw = hg * dh
    spread = (lax.broadcasted_iota(jnp.int32, (hg, gw), 1) // dh
              == lax.broadcasted_iota(jnp.int32, (hg, gw), 0)).astype(F32)
    for g in range(groups):
        hsl = slice(g * hg, (g + 1) * hg)
        cm = c_ref[0, :, g * n_st:(g + 1) * n_st]
        bm = b_ref[0, :, g * n_st:(g + 1) * n_st]
        cb = _dot_nt(cm, bm)
        cum_g = cum_col[:, hsl]
        dt_x = _dot_sel(col[:, hsl], spread)
        ecc_x = _dot_sel(jnp.exp(cum_g), spread)
        eend_x = _dot_sel(jnp.exp(tot[:, hsl] - cum_g), spread)
        v = x_ref[0, :, g * gw:(g + 1) * gw] * dt_x
        hst = st_ref[g]
        y_off = _dot_nt(cm, hst) * ecc_x
        upd = _dot_tn(v * eend_x, bm)
        for h in range(hg):
            ha = g * hg + h
            sl = slice(h * dh, (h + 1) * dh)
            n_rk = cb * jnp.exp(jnp.where(incl, cum_col[:, ha:ha + 1] - cum_row[ha:ha + 1], NEG_BIG))
            o_ref[0, 0, :, g * gw + h * dh:g * gw + (h + 1) * dh] = y_off[:, sl] + _dot(n_rk, v[:, sl])
            st_ref[g, sl, :] = hst[sl] * jnp.exp(tot[:, ha:ha + 1]) + upd[sl]


def ssd_scan(cm, bm, x, dt, da, n_ctx, col_blocks=(0, 0, 0)):
    bn, l_len, _ = x.shape
    heads = dt.shape[-1]
    dh, n_st, grp = SSD_HEAD_DIM, SSD_STATE, SSD_GROUPS
    cb0, bb0, xb0 = col_blocks
    c = SCAN_CHUNK
    nct, ncc = l_len // c, n_ctx // c
    col = jnp.concatenate([dt, da], -1)
    row = jnp.swapaxes(da.reshape(2, bn, nct, c, heads), 3, 4)
    tb = lambda d, j: _time_block(d, j, ncc, nct)
    gseq = lambda off: pl.BlockSpec((1, c, grp * n_st), lambda b, d, j: (b, tb(d, j), off))
    return pl.pallas_call(
        functools.partial(_ssd_scan_kernel, heads=heads, groups=grp, dh=dh, n_st=n_st),
        out_shape=jax.ShapeDtypeStruct((2, bn, l_len, heads * dh), F32),
        grid=(bn, 2, nct),
        in_specs=[gseq(cb0), gseq(bb0),
                  pl.BlockSpec((1, c, heads * dh), lambda b, d, j: (b, tb(d, j), xb0)),
                  pl.BlockSpec((1, 1, c, 2 * heads), lambda b, d, j: (d, b, tb(d, j), 0)),
                  pl.BlockSpec((1, 1, 1, heads, c), lambda b, d, j: (d, b, tb(d, j), 0, 0))],
        out_specs=pl.BlockSpec((1, 1, c, heads * dh), lambda b, d, j: (d, b, tb(d, j), 0)),
        scratch_shapes=[pltpu.VMEM((grp, heads // grp * dh, n_st), F32)],
        compiler_params=pltpu.CompilerParams(dimension_semantics=("parallel", "parallel", "arbitrary")),
        name="ssd_scan",
    )(cm, bm, x, col, row)


def _rwkv_scan_kernel(r_ref, k_ref, v_ref, kk_ref, b_ref, lw_ref, o_ref, st_ref, *, heads, dh):
    d = pl.program_id(2)

    @pl.when(pl.program_id(3) == 0)
    def _():
        st_ref[...] = jnp.zeros_like(st_ref)

    c = r_ref.shape[1]
    incl, strict = _order_masks(c, d)
    lw_all = lw_ref[0, 0]
    cl_all = _sel_dot(incl, lw_all)
    tot_all = jnp.sum(lw_all, 0, keepdims=True)
    hs = range(heads)
    sls = [slice(h * dh, (h + 1) * dh) for h in hs]
    args = [[] for _ in range(6)]
    grams = []
    for h in hs:
        sl = sls[h]
        r, k, v, kk, b = r_ref[0, :, sl], k_ref[0, :, sl], v_ref[0, :, sl], kk_ref[0, :, sl], b_ref[0, :, sl]
        lw, cl, tt = lw_all[:, sl], cl_all[:, sl], tot_all[:, sl]
        inv_p = jnp.exp(-cl)
        w_end = jnp.exp(tt - cl)
        a_t = -kk * jnp.exp(cl - lw)
        r_t = r * jnp.exp(cl)
        grams.append((a_t, r_t, b * inv_p, k * inv_p))
        for lst, val in zip(args, (r_t, a_t, v, k * w_end, b * w_end, jnp.exp(tt))):
            lst.append(val)
    gm = [_dot_nt(jnp.concatenate([a, r], 0), jnp.concatenate([b, k], 0)) for a, r, b, k in grams]
    m_ab = [jnp.where(strict, t[:c, :c], 0.0) for t in gm]
    m_ak = [jnp.where(strict, t[:c, c:], 0.0) for t in gm]
    n_rb = [jnp.where(incl, t[c:, :c], 0.0) for t in gm]
    n_rk = [jnp.where(incl, t[c:, c:], 0.0) for t in gm]
    ys, hts = _scan_chunk([st_ref[h] for h in hs], *args, n_rk, n_rb, m_ab, m_ak)
    for h in hs:
        st_ref[h] = hts[h]
        o_ref[0, 0, :, sls[h]] = ys[h]


def rwkv_scan(r, k, v, kk, b, lw, n_ctx, heads_per_step=32):
    bn, l_len, w = r.shape
    dh = RW_HEAD_DIM
    hb = heads_per_step
    c = SCAN_CHUNK
    nct, ncc = l_len // c, n_ctx // c
    tb = lambda d, j: _time_block(d, j, ncc, nct)
    seq = pl.BlockSpec((1, c, hb * dh), lambda b_, g, d, j: (b_, tb(d, j), g))
    return pl.pallas_call(
        functools.partial(_rwkv_scan_kernel, heads=hb, dh=dh),
        out_shape=jax.ShapeDtypeStruct((2, bn, l_len, w), F32),
        grid=(bn, w // (hb * dh), 2, nct),
        in_specs=[seq, seq, seq, seq, seq,
                  pl.BlockSpec((1, 1, c, hb * dh), lambda b_, g, d, j: (d, b_, tb(d, j), g))],
        out_specs=pl.BlockSpec((1, 1, c, hb * dh), lambda b_, g, d, j: (d, b_, tb(d, j), g)),
        scratch_shapes=[pltpu.VMEM((hb, dh, dh), F32)],
        compiler_params=pltpu.CompilerParams(
            dimension_semantics=("parallel", "parallel", "parallel", "arbitrary")),
        name="rwkv_scan",
    )(r, k, v, kk, b, lw)


def _router_kernel(h_ref, wr_ref, bias_ref, rank_ref, wt_ref, cnt_ref):
    tm = h_ref.shape[0]
    ne, ng = N_EXPERTS, N_GROUPS
    per = ne // ng
    logits = _dot_nt(wr_ref[...], h_ref[...].astype(F32), HIGHEST)
    scores = jax.nn.sigmoid(logits)
    sel = scores + bias_ref[...]
    sel3 = sel.reshape(ng, per, tm)
    idx3 = lax.broadcasted_iota(jnp.int32, (ng, per, tm), 1)
    m1 = jnp.max(sel3, 1, keepdims=True)
    first = jnp.min(jnp.where(sel3 == m1, idx3, per), 1, keepdims=True)
    m2 = jnp.max(jnp.where(idx3 == first, -jnp.inf, sel3), 1, keepdims=True)
    grp = (m1 + m2).reshape(ng, tm)
    gidx = lax.broadcasted_iota(jnp.int32, (ng, tm), 0)
    chosen = jnp.zeros((ng, tm), jnp.bool_)
    for _ in range(TOPK_GROUPS):
        gm = jnp.max(grp, 0, keepdims=True)
        gi = jnp.min(jnp.where(grp == gm, gidx, ng), 0, keepdims=True)
        hit = gidx == gi
        chosen = chosen | hit
        grp = jnp.where(hit, -jnp.inf, grp)
    selm = jnp.where(chosen.reshape(ng, 1, tm), sel3, -jnp.inf).reshape(ne, tm)
    eidx = lax.broadcasted_iota(jnp.int32, (ne, tm), 0)
    picked = jnp.zeros((ne, tm), jnp.bool_)
    for _ in range(TOP_K):
        em = jnp.max(selm, 0, keepdims=True)
        ei = jnp.min(jnp.where(selm == em, eidx, ne), 0, keepdims=True)
        hit = eidx == ei
        picked = picked | hit
        selm = jnp.where(hit, -jnp.inf, selm)
    wsel = jnp.where(picked, scores, 0.0)
    wt_ref[...] = wsel / jnp.sum(wsel, 0, keepdims=True) * ROUTED_SCALE
    onehot = picked.astype(BF16)
    before = (lax.broadcasted_iota(jnp.int32, (tm, tm), 0) < lax.broadcasted_iota(jnp.int32, (tm, tm), 1))
    rank = _dot(onehot, before.astype(BF16))
    rank_ref[...] = jnp.where(picked, rank, -1.0)
    cnt = jnp.sum(picked.astype(F32), 1, keepdims=True)
    cnt_ref[0] = jnp.broadcast_to(cnt, (ne, LANES)).astype(jnp.int32)


def _expert_kernel(cnt_ref, h_ref, rank_ref, wt_ref, wg_ref, wu_ref, wd_ref, sg_ref, su_ref, sd_ref, o_ref):
    i, e = pl.program_id(0), pl.program_id(1)
    tm = h_ref.shape[0]
    rows = 160
    h = h_ref[...]

    d = o_ref.shape[1]
    cw = 512

    @pl.when(e == 0)
    def _():
        act = (jax.nn.silu(_dot(h, sg_ref[...])) * _dot(h, su_ref[...])).astype(BF16)
        for c0 in range(0, d, cw):
            o_ref[:, c0:c0 + cw] = _dot(act, sd_ref[:, c0:c0 + cw])

    rk = rank_ref[0]
    wt = wt_ref[0]
    n_blk = (cnt_ref[i * N_EXPERTS + e] + rows - 1) // rows
    r_iota = lax.broadcasted_iota(jnp.int32, (rows, tm), 0).astype(F32)

    def body(b, carry):
        hit = rk == r_iota + (b * rows).astype(F32)
        xg = _dot(hit.astype(BF16), h).astype(BF16)
        act = jax.nn.silu(_dot(xg, wg_ref[0])) * _dot(xg, wu_ref[0])
        y = _dot(act.astype(BF16), wd_ref[0]).astype(BF16)
        gw = jnp.where(hit, wt, 0.0).astype(BF16)
        for c0 in range(0, d, cw):
            o_ref[:, c0:c0 + cw] += _dot_tn(gw, y[:, c0:c0 + cw])
        return carry

    lax.fori_loop(0, n_blk, body, 0)


def moe(h, w_router, router_bias, we_gate, we_up, we_down, ws_gate, ws_up, ws_down, tm=1024):
    n_tok, d = h.shape
    ne = N_EXPERTS
    nt = n_tok // tm
    ff = we_gate.shape[-1]
    rank, wt, cnt = pl.pallas_call(
        _router_kernel,
        out_shape=[jax.ShapeDtypeStruct((ne, n_tok), F32), jax.ShapeDtypeStruct((ne, n_tok), F32),
                   jax.ShapeDtypeStruct((nt, ne, LANES), jnp.int32)],
        grid=(nt,),
        in_specs=[pl.BlockSpec((tm, d), lambda i: (i, 0)), pl.BlockSpec((ne, d), lambda i: (0, 0)),
                  pl.BlockSpec((ne, 1), lambda i: (0, 0))],
        out_specs=[pl.BlockSpec((ne, tm), lambda i: (0, i)), pl.BlockSpec((ne, tm), lambda i: (0, i)),
                   pl.BlockSpec((1, ne, LANES), lambda i: (i, 0, 0))],
        compiler_params=pltpu.CompilerParams(dimension_semantics=("parallel",), vmem_limit_bytes=VMEM_LIMIT),
        name="moe_router",
    )(h, w_router.T.astype(F32), router_bias.astype(F32).reshape(ne, 1))
    counts = cnt[:, :, 0].reshape(-1)
    const = lambda i, e, c: (0, 0)
    once = pl.Buffered(1)
    return pl.pallas_call(
        _expert_kernel,
        out_shape=jax.ShapeDtypeStruct((n_tok, d), F32),
        grid_spec=pltpu.PrefetchScalarGridSpec(
            num_scalar_prefetch=1,
            grid=(nt, ne),
            in_specs=[pl.BlockSpec((tm, d), lambda i, e, c: (i, 0)),
                      pl.BlockSpec((1, 1, tm), lambda i, e, c: (e, 0, i)),
                      pl.BlockSpec((1, 1, tm), lambda i, e, c: (e, 0, i)),
                      pl.BlockSpec((1, d, ff), lambda i, e, c: (e, 0, 0)),
                      pl.BlockSpec((1, d, ff), lambda i, e, c: (e, 0, 0)),
                      pl.BlockSpec((1, ff, d), lambda i, e, c: (e, 0, 0)),
                      pl.BlockSpec(ws_gate.shape, const, pipeline_mode=once),
                      pl.BlockSpec(ws_up.shape, const, pipeline_mode=once),
                      pl.BlockSpec(ws_down.shape, const, pipeline_mode=once)],
            out_specs=pl.BlockSpec((tm, d), lambda i, e, c: (i, 0))),
        compiler_params=pltpu.CompilerParams(dimension_semantics=("parallel", "arbitrary"),
                                             vmem_limit_bytes=VMEM_LIMIT),
        name="moe_experts",
    )(counts, h, rank.reshape(ne, 1, n_tok), wt.reshape(ne, 1, n_tok),
      we_gate.astype(BF16), we_up.astype(BF16), we_down.astype(BF16),
      ws_gate.astype(BF16), ws_up.astype(BF16), ws_down.astype(BF16))


def _segments(t, n_ctx, fn):
    return jnp.concatenate([fn(t[:, :n_ctx]), fn(t[:, n_ctx:])], 1)


def _dw_conv(t, w):
    pad = w.shape[0] // 2
    return lax.conv_general_dilated(
        t, w[:, None, :].astype(t.dtype), window_strides=(1,), padding=((pad, pad),),
        dimension_numbers=('NWC', 'WIO', 'NWC'), feature_group_count=t.shape[-1])


def _l2n(t):
    return t * lax.rsqrt(jnp.sum(jnp.square(t), -1, keepdims=True) + RMS_EPS)


def _heads(t, n):
    return t.reshape(*t.shape[:-1], n, t.shape[-1] // n)


def _pad_cols(w, width):
    return jnp.pad(w, ((0, 0), (0, width - w.shape[1])))


def _pad_rows(w, height):
    return jnp.pad(w, ((0, height - w.shape[0]), (0, 0)))


def _mod_vectors(c, c_ctx, w_mod, b_mod):
    bn, d = c.shape
    cc = jnp.concatenate([c, c_ctx[None], jnp.zeros((8 - bn - 1, d), F32)], 0)
    mv = matmul([(jax.nn.silu(cc), w_mod.astype(BF16))]) + b_mod
    mv = mv.reshape(8, 6, d)
    lat = mv[:bn]
    ctx = jnp.broadcast_to(mv[bn][None], lat.shape)
    return jnp.stack([ctx, lat], 1)


def _pack_mv(gate, shift, scale):
    z = jnp.zeros_like(gate)
    return jnp.stack([gate, shift, scale, z, z, z, z, z], 2)


def _cat_pad(parts):
    return jnp.concatenate([_pad_cols(a, wd) if a.ndim == 2 else jnp.pad(a, (0, wd - a.shape[0]))
                            for a, wd in parts], -1)


def _seg_allsum(x, seg):
    if seg == LANES:
        parts = [x[:, c0:c0 + LANES] for c0 in range(0, x.shape[-1], LANES)]
        parts = [jnp.broadcast_to(jnp.sum(t, -1, keepdims=True), t.shape) for t in parts]
        return parts[0] if len(parts) == 1 else jnp.concatenate(parts, -1)
    bw = 2 * LANES
    same = (lax.broadcasted_iota(jnp.int32, (bw, bw), 0) // seg) == (lax.broadcasted_iota(jnp.int32, (bw, bw), 1) // seg)
    ones_bd = jnp.where(same, 1.0, 0.0).astype(BF16)
    hi, mid, lo = _split3(x)
    parts = []
    for c0 in range(0, x.shape[-1], bw):
        sl = slice(c0, c0 + bw)
        parts.append(_dot(hi[:, sl], ones_bd) + _dot(mid[:, sl], ones_bd) + _dot(lo[:, sl], ones_bd))
    return parts[0] if len(parts) == 1 else jnp.concatenate(parts, -1)


def _halo_rows(x, xp_ref, xn_ref, i, n_ctx_tiles):
    has_prev = (i > 0) & (i != n_ctx_tiles)
    has_next = (i != n_ctx_tiles - 1) & (i != pl.num_programs(1) - 1)
    xp = jnp.where(has_prev, xp_ref[0], 0.0)
    xn = jnp.where(has_next, xn_ref[0], 0.0)
    return jnp.concatenate([xp, x, xn], 0)


def _halo_specs(tr, width, l_len, col_block):
    nh, n8 = tr // 8, l_len // 8
    return [pl.BlockSpec((1, tr, width), lambda b, i, *s: (b, i, col_block(*s))),
            pl.BlockSpec((1, 8, width), lambda b, i, *s: (b, jnp.maximum(i * nh - 1, 0), col_block(*s))),
            pl.BlockSpec((1, 8, width), lambda b, i, *s: (b, jnp.minimum((i + 1) * nh, n8 - 1), col_block(*s)))]


def _conv_kernel(x_ref, xp_ref, xn_ref, w_ref, b_ref, o_ref, *, n_ctx_tiles, kw, l2_segments):
    i, s = pl.program_id(1), pl.program_id(2)
    x = x_ref[0]
    tr = x.shape[0]
    xe = _halo_rows(x, xp_ref, xn_ref, i, n_ctx_tiles)
    pad = kw // 2
    acc = jnp.broadcast_to(b_ref[...], x.shape)
    for j in range(kw):
        acc = acc + w_ref[j:j + 1, :] * xe[8 - pad + j:8 - pad + j + tr]
    y = acc * jax.nn.sigmoid(acc)
    if l2_segments:
        yn = y * lax.rsqrt(_seg_allsum(y * y, DN_HEAD_DIM) + RMS_EPS)
        yn = yn * jnp.where(s == 0, DN_HEAD_DIM ** -0.5, 1.0)
        y = jnp.where(s < 2, yn, y)
    o_ref[0] = y


def conv_silu(p, col0, width, seg_w, conv_w, conv_b, n_ctx, l2_segments=False):
    bn, l_len, _ = p.shape
    tr = 256
    cb = col0 // seg_w
    return pl.pallas_call(
        functools.partial(_conv_kernel, n_ctx_tiles=n_ctx // tr, kw=conv_w.shape[0], l2_segments=l2_segments),
        out_shape=jax.ShapeDtypeStruct((bn, l_len, width), F32),
        grid=(bn, l_len // tr, width // seg_w),
        in_specs=_halo_specs(tr, seg_w, l_len, lambda s: cb + s)
        + [pl.BlockSpec((8, seg_w), lambda b, i, s: (0, s)), pl.BlockSpec((1, seg_w), lambda b, i, s: (0, s))],
        out_specs=pl.BlockSpec((1, tr, seg_w), lambda b, i, s: (b, i, s)),
        compiler_params=pltpu.CompilerParams(dimension_semantics=("parallel", "parallel", "parallel"),
                                             vmem_limit_bytes=VMEM_LIMIT),
        name="conv_silu",
    )(p, p, p, _pad_rows(conv_w.astype(F32), 8), conv_b.astype(F32).reshape(1, width))


def _softplus(x):
    return jnp.maximum(x, 0.0) + jnp.log1p(jnp.exp(-jnp.abs(x)))


def _rwkv_prep_kernel(x_ref, xp_ref, xn_ref, mu_ref, vec_ref, a2_ref, g2_ref, w2f_ref, w2b_ref,
                      r_ref, k_ref, v_ref, kk_ref, b_ref, lw_ref, g_ref, *, n_ctx_tiles):
    x = x_ref[0]
    tr = x.shape[0]
    xe = _halo_rows(x, xp_ref, xn_ref, pl.program_id(1), n_ctx_tiles)
    x = x + mu_ref[...] * (0.5 * (xe[7:7 + tr] + xe[9:9 + tr]) - x)
    w = RW_W
    r, k, v = x[:, :w], x[:, w:2 * w], x[:, 2 * w:3 * w]
    o = 3 * w
    wl_f, wl_b, al = x[:, o:o + LANES], x[:, o + LANES:o + 2 * LANES], x[:, o + 2 * LANES:o + 3 * LANES]
    gl = x[:, o + 3 * LANES:o + 3 * LANES + RW_GATE_LORA]
    vec = vec_ref[...]
    a = jax.nn.sigmoid(vec[0:1] + _dot(al, a2_ref[...]))
    g_ref[0] = _dot(jax.nn.sigmoid(gl), g2_ref[...])
    kk = k * vec[1:2]
    kk = kk * lax.rsqrt(_seg_allsum(kk * kk, RW_HEAD_DIM) + RMS_EPS)
    r_ref[0] = r
    v_ref[0] = v
    k_ref[0] = k * (1.0 + (a - 1.0) * vec[2:3])
    kk_ref[0] = kk
    b_ref[0] = kk * a
    for di, (wl, w2_ref) in enumerate(((wl_f, w2f_ref), (wl_b, w2b_ref))):
        w_log = -_softplus(-(vec[3 + di:4 + di] + _dot(jnp.tanh(wl), w2_ref[...]))) - 0.5
        lw_ref[di, 0] = -jnp.exp(w_log)


def rwkv_prep(p_rw, mu, vec, a2, g2, w2f, w2b, n_ctx):
    bn, l_len, wp = p_rw.shape
    tr = 128
    w = RW_W
    full = lambda a: pl.BlockSpec(a.shape, lambda b, i: (0,) * a.ndim)
    seq = pl.BlockSpec((1, tr, w), lambda b, i: (b, i, 0))
    sds = jax.ShapeDtypeStruct((bn, l_len, w), F32)
    return pl.pallas_call(
        functools.partial(_rwkv_prep_kernel, n_ctx_tiles=n_ctx // tr),
        out_shape=[sds, sds, sds, sds, sds, jax.ShapeDtypeStruct((2, bn, l_len, w), F32), sds],
        grid=(bn, l_len // tr),
        in_specs=_halo_specs(tr, wp, l_len, lambda: 0) + [full(mu), full(vec), full(a2), full(g2), full(w2f), full(w2b)],
        out_specs=[seq, seq, seq, seq, seq, pl.BlockSpec((2, 1, tr, w), lambda b, i: (0, b, i, 0)), seq],
        compiler_params=pltpu.CompilerParams(dimension_semantics=("parallel", "parallel"),
                                             vmem_limit_bytes=VMEM_LIMIT),
        name="rwkv_prep",
    )(p_rw, p_rw, p_rw, mu, vec, a2, g2, w2f, w2b)


def _l0_out_kernel(att_ref, of_ref, ob_ref, zg_ref, nw_ref, w_ref, z_ref, mv_ref, g_ref, b_ref, zn_ref, h_ref):
    o = of_ref[0] + ob_ref[0]
    o = o * lax.rsqrt(_seg_allsum(o * o, DN_HEAD_DIM) * (1.0 / DN_HEAD_DIM) + RMS_EPS) * nw_ref[...]
    zg = zg_ref[0]
    dn = o * (zg * jax.nn.sigmoid(zg))
    mix = _dot(att_ref[0], w_ref[:A_Q]) + _dot(dn, w_ref[A_Q:])
    _residual_ln_mod(z_ref[0], mix, mv_ref[0, 0], g_ref[...], b_ref[...], zn_ref, h_ref)


def _l1_out_kernel(yf_ref, yb_ref, xs_ref, zs_ref, rf_ref, rb_ref, r_ref, k_ref, v_ref, gt_ref, vec_ref, w_ref,
                   z_ref, mv_ref, g_ref, b_ref, zn_ref, h_ref):
    vec = vec_ref[...]
    zs = zs_ref[0]
    y = (yf_ref[0, 0] + yb_ref[0, 0] + vec[0:1] * xs_ref[0]) * (zs * jax.nn.sigmoid(zs))
    gw = SSD_W // SSD_GROUPS
    parts = []
    for gi in range(SSD_GROUPS):
        yg = y[:, gi * gw:(gi + 1) * gw]
        parts.append(yg * lax.rsqrt(jnp.mean(yg * yg, -1, keepdims=True) + RMS_EPS))
    ssd = jnp.concatenate(parts, -1) * vec[1:2]
    yr = rf_ref[0, 0] + rb_ref[0, 0]
    inv = 1.0 / RW_HEAD_DIM
    yc = yr - _seg_allsum(yr, RW_HEAD_DIM) * inv
    var = _seg_allsum(yc * yc, RW_HEAD_DIM) * inv
    yr = yc * lax.rsqrt(var + RW_LN_EPS) * vec[2:3] + vec[3:4]
    yr = yr + _seg_allsum(r_ref[0] * k_ref[0] * vec[4:5], RW_HEAD_DIM) * v_ref[0]
    rw = yr * gt_ref[0]
    mix = _dot(ssd, w_ref[:SSD_W]) + _dot(rw, w_ref[SSD_W:])
    _residual_ln_mod(z_ref[0], mix, mv_ref[0, 0], g_ref[...], b_ref[...], zn_ref, h_ref)


def _mixer_out_call(body, name, tr, seq_inputs, vec_inputs, w_out, z, mv, ln_g, ln_b, n_ctx):
    bn, l_len, d = z.shape
    nbc = n_ctx // tr
    full = lambda a: pl.BlockSpec(a.shape, lambda b, i: (0,) * a.ndim)
    row = pl.BlockSpec((1, tr, d), lambda b, i: (b, i, 0))
    w_b = w_out.astype(BF16)
    return pl.pallas_call(
        body,
        out_shape=[jax.ShapeDtypeStruct((bn, l_len, d), F32), jax.ShapeDtypeStruct((bn, l_len, d), BF16)],
        grid=(bn, l_len // tr),
        in_specs=[pl.BlockSpec(bs, im) for _, bs, im in seq_inputs] + [full(a) for a in vec_inputs]
        + [pl.BlockSpec(w_b.shape, lambda b, i: (0, 0), pipeline_mode=pl.Buffered(1)), row,
           pl.BlockSpec((1, 1, 8, d), lambda b, i: (b, jnp.where(i < nbc, 0, 1), 0, 0)),
           pl.BlockSpec((1, d), lambda b, i: (0, 0)), pl.BlockSpec((1, d), lambda b, i: (0, 0))],
        out_specs=[row, row],
        compiler_params=pltpu.CompilerParams(dimension_semantics=("parallel", "parallel"),
                                             vmem_limit_bytes=VMEM_LIMIT),
        name=name,
    )(*[a for a, _, _ in seq_inputs], *vec_inputs, w_b, z, mv, ln_g.reshape(1, d), ln_b.reshape(1, d))


def _layer0(z, h, mv, n_ctx, w_in, w_out, attn_sink, conv_w, a_log, dt_bias, norm_w, ln_g, ln_b, cs):
    bn, l_len, d = h.shape
    q, k, v, dq, dk, dv, zz, sm = jnp.split(
        w_in, np.cumsum([A_Q, A_KV, A_KV, DN_W, DN_W, DN_W, DN_W]).tolist(), 1)
    w_in_p = jnp.concatenate([q, dq, dk, dv, zz, k, v, _pad_cols(sm, LANES)], 1).astype(BF16)
    k_col = A_Q + 4 * DN_W
    p = matmul([(h.reshape(bn * l_len, d), w_in_p)]).reshape(bn, l_len, -1)
    att = attention(p, cs, attn_sink, n_ctx, 0, k_col, k_col + A_KV)
    qkv = conv_silu(p, A_Q, 3 * DN_W, DN_W, conv_w, jnp.zeros((3 * DN_W,), F32), n_ctx, l2_segments=True)
    sm_col = k_col + 2 * A_KV
    sm = p[..., sm_col:sm_col + 3 * DN_HEADS]
    a_f, a_b, bb = sm[..., :DN_HEADS], sm[..., DN_HEADS:2 * DN_HEADS], sm[..., 2 * DN_HEADS:]
    beta = jax.nn.sigmoid(bb)
    g = jnp.stack([-jnp.exp(a_log[0]) * jax.nn.softplus(a_f + dt_bias[0]),
                   -jnp.exp(a_log[1]) * jax.nn.softplus(a_b + dt_bias[1])], 0)
    o = dn_scan(qkv, qkv, qkv, beta, g, n_ctx, col_blocks=(0, 1, 2))
    tr = 256
    seq = [(att, (1, tr, A_Q), lambda b, i: (b, i, 0)),
           (o[0], (1, tr, DN_W), lambda b, i: (b, i, 0)),
           (o[1], (1, tr, DN_W), lambda b, i: (b, i, 0)),
           (p, (1, tr, DN_W), lambda b, i: (b, i, (A_Q + 3 * DN_W) // DN_W))]
    return _mixer_out_call(_l0_out_kernel, "l0_out", tr, seq, [jnp.tile(norm_w, DN_HEADS).reshape(1, DN_W)],
                           w_out, z, mv, ln_g, ln_b, n_ctx)


def _layer1(z, h, mv, n_ctx, w_in, w_out, ssd_conv_w, ssd_conv_b, ssd_a_log, ssd_dt_bias, ssd_d, ssd_norm_w,
            rw_mu, rw_w0, rw_w2, rw_a0, rw_a2, rw_g2, rw_k_k, rw_k_a, rw_r_k, rw_ln_g, rw_ln_b, ln_g, ln_b):
    bn, l_len, d = h.shape
    n_tok = bn * l_len
    ssd_in = 2 * SSD_W + 2 * SSD_GN + 2 * SSD_HEADS
    lora_w = [RW_DECAY_LORA, RW_DECAY_LORA, RW_A_LORA]
    h2 = h.reshape(n_tok, d)
    w_ssd = _pad_cols(w_in[:, :ssd_in], 5120).astype(BF16)
    p_ssd = matmul([(h2, w_ssd)]).reshape(bn, l_len, -1)
    xbc = conv_silu(p_ssd, SSD_W, SSD_W + 2 * SSD_GN, 512, ssd_conv_w, ssd_conv_b, n_ctx)
    dtc = p_ssd[..., 2 * SSD_W + 2 * SSD_GN:ssd_in]
    dt = jnp.stack([jax.nn.softplus(dtc[..., :SSD_HEADS] + ssd_dt_bias[0]),
                    jax.nn.softplus(dtc[..., SSD_HEADS:] + ssd_dt_bias[1])], 0)
    da = -jnp.exp(ssd_a_log)[:, None, None, :] * dt
    y_ssd = ssd_scan(xbc, xbc, xbc, dt, da, n_ctx,
                     col_blocks=((SSD_W + SSD_GN) // SSD_GN, SSD_W // SSD_GN, 0))
    splits = np.cumsum([RW_W, RW_W, RW_W] + lora_w).tolist()
    widths = [RW_W, RW_W, RW_W, LANES, LANES, LANES, RW_GATE_LORA]
    w_rw = _cat_pad(list(zip(jnp.split(w_in[:, ssd_in:], splits, 1), widths)))
    w_rw = _pad_cols(w_rw, 6912).astype(BF16)
    mu = _cat_pad(list(zip(jnp.split(rw_mu, splits), widths)))
    mu = jnp.pad(mu, (0, w_rw.shape[1] - mu.shape[0])).reshape(1, -1)
    p_rw = matmul([(h2, w_rw)]).reshape(bn, l_len, -1)
    zero = jnp.zeros((RW_W,), F32)
    vec = jnp.stack([rw_a0, rw_k_k, rw_k_a, rw_w0[0], rw_w0[1], zero, zero, zero], 0)
    lora_p = lambda w: _pad_rows(w, LANES).astype(BF16)
    r, k, v, kk, b, lw, g = rwkv_prep(p_rw, mu, vec, lora_p(rw_a2), rw_g2.astype(BF16), lora_p(rw_w2[0]),
                                      lora_p(rw_w2[1]), n_ctx)
    y_rw = rwkv_scan(r, k, v, kk, b, lw, n_ctx)
    tr = 128
    rowmap = lambda b_, i: (b_, i, 0)
    seq = [(y_ssd, (1, 1, tr, SSD_W), lambda b_, i: (0, b_, i, 0)),
           (y_ssd, (1, 1, tr, SSD_W), lambda b_, i: (1, b_, i, 0)),
           (xbc, (1, tr, SSD_W), rowmap), (p_ssd, (1, tr, SSD_W), rowmap),
           (y_rw, (1, 1, tr, RW_W), lambda b_, i: (0, b_, i, 0)),
           (y_rw, (1, 1, tr, RW_W), lambda b_, i: (1, b_, i, 0)),
           (r, (1, tr, RW_W), rowmap), (k, (1, tr, RW_W), rowmap), (v, (1, tr, RW_W), rowmap),
           (g, (1, tr, RW_W), rowmap)]
    vec_out = jnp.stack([jnp.repeat(ssd_d, SSD_HEAD_DIM), ssd_norm_w, rw_ln_g, rw_ln_b, rw_r_k.reshape(-1),
                         zero, zero, zero], 0)
    return _mixer_out_call(_l1_out_kernel, "l1_out", tr, seq, [vec_out], w_out, z, mv, ln_g, ln_b, n_ctx)


def kernel(x, c, ctx, c_ctx, l0_w_mod, l0_b_mod, l0_ln1_g, l0_ln1_b, l0_ln2_g, l0_ln2_b, l0_w_in, l0_w_out, l0_attn_sink, l0_dn_conv_w, l0_dn_a_log, l0_dn_dt_bias, l0_dn_norm_w, l0_w_router, l0_router_bias, l0_we_gate, l0_we_up, l0_we_down, l0_ws_gate, l0_ws_up, l0_ws_down, l1_w_mod, l1_b_mod, l1_ln1_g, l1_ln1_b, l1_ln2_g, l1_ln2_b, l1_w_in, l1_w_out, l1_ssd_conv_w, l1_ssd_conv_b, l1_ssd_a_log, l1_ssd_dt_bias, l1_ssd_d, l1_ssd_norm_w, l1_rw_mu, l1_rw_w0, l1_rw_w2, l1_rw_a0, l1_rw_a2, l1_rw_g2, l1_rw_k_k, l1_rw_k_a, l1_rw_r_k, l1_rw_ln_g, l1_rw_ln_b, l1_w_router, l1_router_bias, l1_we_gate, l1_we_up, l1_we_down, l1_ws_gate, l1_ws_up, l1_ws_down):
    bn, s_len, d = x.shape
    n_ctx = ctx.shape[1]
    l_len = n_ctx + s_len
    n_tok = bn * l_len
    z = jnp.concatenate([ctx, x], 1)
    cs = rope_table(n_ctx, s_len)
    mv0 = _mod_vectors(c, c_ctx, l0_w_mod, l0_b_mod)
    mv1 = _mod_vectors(c, c_ctx, l1_w_mod, l1_b_mod)
    pick = lambda mv, gi, shi, sci: _pack_mv(mv[:, :, gi], mv[:, :, shi], mv[:, :, sci])

    (h,) = ln_mod(z, z, pick(mv0, 2, 0, 1), l0_ln1_g, l0_ln1_b, n_ctx, do_ln=False, do_mod=True)
    z, h = _layer0(z, h, pick(mv0, 2, 3, 4), n_ctx, l0_w_in, l0_w_out, l0_attn_sink, l0_dn_conv_w, l0_dn_a_log,
                   l0_dn_dt_bias, l0_dn_norm_w, l0_ln1_g, l0_ln1_b, cs)
    f = moe(h.reshape(n_tok, d), l0_w_router, l0_router_bias, l0_we_gate, l0_we_up, l0_we_down,
            l0_ws_gate, l0_ws_up, l0_ws_down).reshape(bn, l_len, d)
    mv_a = _pack_mv(mv0[:, :, 5], mv1[:, :, 0], mv1[:, :, 1])
    z, h = ln_mod(z, f, mv_a, l0_ln2_g, l0_ln2_b, n_ctx, do_ln=True, do_mod=True)

    z, h = _layer1(z, h, pick(mv1, 2, 3, 4), n_ctx, l1_w_in, l1_w_out, l1_ssd_conv_w, l1_ssd_conv_b, l1_ssd_a_log,
                   l1_ssd_dt_bias, l1_ssd_d, l1_ssd_norm_w, l1_rw_mu, l1_rw_w0, l1_rw_w2, l1_rw_a0, l1_rw_a2,
                   l1_rw_g2, l1_rw_k_k, l1_rw_k_a, l1_rw_r_k, l1_rw_ln_g, l1_rw_ln_b, l1_ln1_g, l1_ln1_b)
    f = moe(h.reshape(n_tok, d), l1_w_router, l1_router_bias, l1_we_gate, l1_we_up, l1_we_down,
            l1_ws_gate, l1_ws_up, l1_ws_down).reshape(bn, l_len, d)
    (zx,) = ln_mod(z, f, pick(mv1, 5, 3, 4), l1_ln2_g, l1_ln2_b, n_ctx, do_ln=True, do_mod=False, lat_only=True)
    return zx
```

```python
import functools

import jax
import jax.numpy as jnp
import numpy as np
from jax import lax
from jax.experimental import pallas as pl
from jax.experimental.pallas import tpu as pltpu

F32 = jnp.float32
BF16 = jnp.bfloat16

DEPTH = 2
GRID_W = 64
HEAD_DIM = 128
A_Q_HEADS = 8
A_KV_HEADS = 2
WINDOW = 128
ATT_BLOCK = 128
ROPE_THETA = 10000.0
DN_HEADS = 8
DN_HEAD_DIM = 128
SSD_HEADS = 32
SSD_HEAD_DIM = 64
SSD_GROUPS = 2
SSD_STATE = 128
RW_HEADS = 32
RW_HEAD_DIM = 64
RW_DECAY_LORA = 96
RW_A_LORA = 96
RW_GATE_LORA = 256
RW_LN_EPS = 64e-5
N_EXPERTS = 64
TOP_K = 8
N_GROUPS = 8
TOPK_GROUPS = 4
ROUTED_SCALE = 2.5
LN_EPS = 1e-5
RMS_EPS = 1e-6
DEEPNORM_ALPHA = (2 * DEPTH) ** 0.25

A_Q = A_Q_HEADS * HEAD_DIM
A_KV = A_KV_HEADS * HEAD_DIM
DN_W = DN_HEADS * DN_HEAD_DIM
SSD_W = SSD_HEADS * SSD_HEAD_DIM
SSD_GN = SSD_GROUPS * SSD_STATE
RW_W = RW_HEADS * RW_HEAD_DIM

LANES = 128
SCAN_CHUNK = 64
VMEM_LIMIT = 56 * 1024 * 1024
NEG_BIG = -1e30

HIGHEST = lax.Precision.HIGHEST


def _dot_dims(a, b, dims, precision):
    if precision is None:
        a, b = a.astype(BF16), b.astype(BF16)
    return lax.dot_general(a, b, (dims, ((), ())), preferred_element_type=F32, precision=precision)


def _dot(a, b, precision=None):
    return _dot_dims(a, b, ((1,), (0,)), precision)


def _dot_nt(a, b, precision=None):
    return _dot_dims(a, b, ((1,), (1,)), precision)


def _dot_tn(a, b, precision=None):
    return _dot_dims(a, b, ((0,), (0,)), precision)


def _split3(x):
    hi = x.astype(BF16)
    r1 = x - hi.astype(F32)
    mid = r1.astype(BF16)
    return hi, mid, (r1 - mid.astype(F32)).astype(BF16)


def _dot_sel(x, sel01):
    s = sel01.astype(BF16)
    hi, mid, lo = _split3(x)
    return _dot(hi, s) + _dot(mid, s) + _dot(lo, s)


def _sel_dot(sel01, x):
    s = sel01.astype(BF16)
    hi, mid, lo = _split3(x)
    return _dot(s, hi) + _dot(s, mid) + _dot(s, lo)


def _pick_tile(n, candidates):
    for c in candidates:
        if n % c == 0:
            return c
    return n


def _mm_kernel(*refs, n_pairs):
    o_ref = refs[2 * n_pairs]
    acc = None
    for a_ref, w_ref in zip(refs[:n_pairs], refs[n_pairs:2 * n_pairs]):
        p = _dot(a_ref[...].astype(BF16), w_ref[...].astype(BF16))
        acc = p if acc is None else acc + p
    o_ref[...] = acc.astype(o_ref.dtype)


def matmul(pairs, out_dtype=F32):
    m = pairs[0][0].shape[0]
    n = pairs[0][1].shape[1]
    tn = _pick_tile(n, (1280, 1152, 1024, 768, 512, 384, 256, 128))
    tm = _pick_tile(m, (1024, 512, 256, 128) if tn <= 768 else (512, 256, 128))
    in_specs = ([pl.BlockSpec((tm, a.shape[1]), lambda j, i: (i, 0)) for a, _ in pairs]
                + [pl.BlockSpec((w.shape[0], tn), lambda j, i: (0, j)) for _, w in pairs])
    return pl.pallas_call(
        functools.partial(_mm_kernel, n_pairs=len(pairs)),
        out_shape=jax.ShapeDtypeStruct((m, n), out_dtype),
        grid=(n // tn, m // tm),
        in_specs=in_specs,
        out_specs=pl.BlockSpec((tm, tn), lambda j, i: (i, j)),
        compiler_params=pltpu.CompilerParams(dimension_semantics=("parallel", "parallel"),
                                             vmem_limit_bytes=VMEM_LIMIT),
        name="matmul",
    )(*[a for a, _ in pairs], *[w for _, w in pairs])


def _residual_ln_mod(z, o, mv, g, b, zn_ref, h_ref):
    t = DEEPNORM_ALPHA * z + mv[0:1] * o
    tc = t - jnp.mean(t, -1, keepdims=True)
    zn = tc * lax.rsqrt(jnp.mean(tc * tc, -1, keepdims=True) + LN_EPS) * g + b
    zn_ref[0] = zn
    h_ref[0] = (zn * (1.0 + mv[2:3]) + mv[1:2]).astype(h_ref.dtype)


def _ln_mod_kernel(z_ref, o_ref, mv_ref, g_ref, b_ref, zn_ref, h_ref, *, do_ln, do_mod):
    z = z_ref[0]
    mv = mv_ref[0, 0]
    if do_ln:
        t = DEEPNORM_ALPHA * z + mv[0:1] * o_ref[0]
        mu = jnp.mean(t, -1, keepdims=True)
        tc = t - mu
        var = jnp.mean(tc * tc, -1, keepdims=True)
        z = tc * lax.rsqrt(var + LN_EPS) * g_ref[...] + b_ref[...]
        zn_ref[0] = z
    if do_mod:
        h_ref[0] = (z * (1.0 + mv[2:3]) + mv[1:2]).astype(h_ref.dtype)


def ln_mod(z, o, mv, ln_g, ln_b, n_ctx, *, do_ln, do_mod, lat_only=False):
    bn, l_len, d = z.shape
    tr = 256
    nbc = n_ctx // tr
    off = nbc if lat_only else 0
    n_out = l_len - off * tr
    row = lambda b, i: (b, i + off, 0)
    outs, out_specs = [], []
    if do_ln:
        outs.append(jax.ShapeDtypeStruct((bn, n_out, d), F32))
        out_specs.append(pl.BlockSpec((1, tr, d), lambda b, i: (b, i, 0)))
    if do_mod:
        outs.append(jax.ShapeDtypeStruct((bn, n_out, d), BF16))
        out_specs.append(pl.BlockSpec((1, tr, d), lambda b, i: (b, i, 0)))

    def body(z_ref, o_ref, mv_ref, g_ref, b_ref, *out_refs):
        zn_ref = out_refs[0] if do_ln else None
        h_ref = out_refs[-1] if do_mod else None
        _ln_mod_kernel(z_ref, o_ref, mv_ref, g_ref, b_ref, zn_ref, h_ref, do_ln=do_ln, do_mod=do_mod)

    res = pl.pallas_call(
        body,
        out_shape=outs,
        grid=(bn, n_out // tr),
        in_specs=[pl.BlockSpec((1, tr, d), row), pl.BlockSpec((1, tr, d), row),
                  pl.BlockSpec((1, 1, 8, d), lambda b, i: (b, jnp.where(i + off < nbc, 0, 1), 0, 0)),
                  pl.BlockSpec((1, d), lambda b, i: (0, 0)), pl.BlockSpec((1, d), lambda b, i: (0, 0))],
        out_specs=out_specs,
        compiler_params=pltpu.CompilerParams(dimension_semantics=("parallel", "parallel"),
                                             vmem_limit_bytes=VMEM_LIMIT),
        name="ln_mod",
    )(z, o, mv, ln_g.reshape(1, d), ln_b.reshape(1, d))
    return res


def _rope(x, cs):
    lane = lax.broadcasted_iota(jnp.int32, x.shape, 1)
    quarter = HEAD_DIM // 4
    first = (lane % (2 * quarter)) < quarter
    partner = jnp.where(first, pltpu.roll(x, HEAD_DIM - quarter, 1), pltpu.roll(x, quarter, 1))
    return x * cs[:, :HEAD_DIM] + partner * cs[:, HEAD_DIM:]


def _attn_kernel(q_ref, kp_ref, kc_ref, kn_ref, vp_ref, vc_ref, vn_ref, kx_ref, vx_ref,
                 csp_ref, csc_ref, csn_ref, sink_ref, o_ref, *, n_ctx_blocks, s_len):
    i = pl.program_id(1)
    blk = ATT_BLOCK
    grp = A_Q_HEADS // A_KV_HEADS
    n_ctx = kx_ref.shape[1]
    csc = csc_ref[...]
    q = q_ref[0]
    q4 = jnp.concatenate([_rope(q[:, h * HEAD_DIM:(h + 1) * HEAD_DIM], csc) for h in range(grp)], 0)
    k_all = jnp.concatenate([_rope(kp_ref[0], csp_ref[...]), _rope(kc_ref[0], csc),
                             _rope(kn_ref[0], csn_ref[...]), kx_ref[0]], 0)
    v_all = jnp.concatenate([vp_ref[0], vc_ref[0], vn_ref[0], vx_ref[0]], 0)
    n_keys = 3 * blk + n_ctx
    s = _dot_nt(q4, k_all) * HEAD_DIM ** -0.5
    s = s.reshape(grp, blk, n_keys)
    tq = lax.broadcasted_iota(jnp.int32, (1, blk, n_keys), 1)
    col = lax.broadcasted_iota(jnp.int32, (1, blk, n_keys), 2)
    koff = col - blk
    pos = (i - n_ctx_blocks) * blk + koff
    local_ok = (jnp.abs(tq - koff) <= WINDOW) & (pos >= 0) & (pos < s_len) & (i >= n_ctx_blocks)
    valid = local_ok | (col >= 3 * blk)
    s = jnp.where(valid, s, -jnp.inf)
    sink = sink_ref[0]
    hsel = lax.broadcasted_iota(jnp.int32, (grp, 1, 1), 0)
    sink3 = jnp.zeros((grp, 1, 1), F32)
    for h in range(grp):
        sink3 = jnp.where(hsel == h, sink[:, h:h + 1].reshape(1, 1, 1), sink3)
    m = jnp.maximum(jnp.max(s, -1, keepdims=True), sink3)
    p = jnp.exp(s - m)
    denom = jnp.sum(p, -1, keepdims=True) + jnp.exp(sink3 - m)
    o = _dot(p.reshape(grp * blk, n_keys), v_all).reshape(grp, blk, HEAD_DIM) / denom
    for h in range(grp):
        o_ref[0, :, h * HEAD_DIM:(h + 1) * HEAD_DIM] = o[h]


def attention(p_all, cs, sink, n_ctx, q_col, k_col, v_col):
    bn, l_len, _ = p_all.shape
    blk = ATT_BLOCK
    grp = A_Q_HEADS // A_KV_HEADS
    nb = l_len // blk
    ncb = n_ctx // blk
    s_len = l_len - n_ctx
    qw = grp * HEAD_DIM
    prev = lambda i: jnp.clip(i - 1, ncb, nb - 1)
    cur = lambda i: jnp.clip(i, ncb, nb - 1)
    nxt = lambda i: jnp.clip(i + 1, ncb, nb - 1)
    kb, vb = k_col // HEAD_DIM, v_col // HEAD_DIM

    def kv_spec(colb, rowf):
        return pl.BlockSpec((1, blk, HEAD_DIM), lambda b, i, h: (b, rowf(i), colb + h))

    def cs_spec(rowf):
        return pl.BlockSpec((blk, 2 * HEAD_DIM), lambda b, i, h: (rowf(i), 0))

    return pl.pallas_call(
        functools.partial(_attn_kernel, n_ctx_blocks=ncb, s_len=s_len),
        out_shape=jax.ShapeDtypeStruct((bn, l_len, A_Q), F32),
        grid=(bn, nb, A_KV_HEADS),
        in_specs=[pl.BlockSpec((1, blk, qw), lambda b, i, h: (b, i, q_col // qw + h)),
                  kv_spec(kb, prev), kv_spec(kb, cur), kv_spec(kb, nxt),
                  kv_spec(vb, prev), kv_spec(vb, cur), kv_spec(vb, nxt),
                  pl.BlockSpec((1, n_ctx, HEAD_DIM), lambda b, i, h: (b, 0, kb + h)),
                  pl.BlockSpec((1, n_ctx, HEAD_DIM), lambda b, i, h: (b, 0, vb + h)),
                  cs_spec(prev), pl.BlockSpec((blk, 2 * HEAD_DIM), lambda b, i, h: (i, 0)), cs_spec(nxt),
                  pl.BlockSpec((1, 1, grp), lambda b, i, h: (h, 0, 0))],
        out_specs=pl.BlockSpec((1, blk, qw), lambda b, i, h: (b, i, h)),
        compiler_params=pltpu.CompilerParams(dimension_semantics=("parallel", "parallel", "parallel"),
                                             vmem_limit_bytes=VMEM_LIMIT),
        name="window_attention",
    )(p_all, p_all, p_all, p_all, p_all, p_all, p_all, p_all, p_all, cs, cs, cs,
      sink.astype(F32).reshape(A_KV_HEADS, 1, grp))


def rope_table(n_ctx, s_len):
    rows = s_len // GRID_W
    row = jnp.repeat(jnp.arange(rows, dtype=F32), GRID_W)
    colp = jnp.tile(jnp.arange(GRID_W, dtype=F32), rows)
    n_freq = HEAD_DIM // 4
    inv_freq = ROPE_THETA ** (-jnp.arange(n_freq, dtype=F32) / n_freq)
    ang_r = row[:, None] * inv_freq
    ang_c = colp[:, None] * inv_freq
    cos = jnp.concatenate([jnp.cos(ang_r), jnp.cos(ang_r), jnp.cos(ang_c), jnp.cos(ang_c)], -1)
    sin = jnp.concatenate([-jnp.sin(ang_r), jnp.sin(ang_r), -jnp.sin(ang_c), jnp.sin(ang_c)], -1)
    lat = jnp.concatenate([cos, sin], -1)
    ctx = jnp.concatenate([jnp.ones((n_ctx, HEAD_DIM), F32), jnp.zeros((n_ctx, HEAD_DIM), F32)], -1)
    return jnp.concatenate([ctx, lat], 0)


def _order_masks(c, d):
    t = lax.broadcasted_iota(jnp.int32, (c, c), 0)
    j = lax.broadcasted_iota(jnp.int32, (c, c), 1)
    rel = (t - j) * (1 - 2 * d)
    return rel >= 0, rel > 0


def _unit_inverses(ms):
    c = ms[0].shape[0]
    eye = (lax.broadcasted_iota(jnp.int32, (c, c), 0) == lax.broadcasted_iota(jnp.int32, (c, c), 1)).astype(F32)
    ts = [eye + m for m in ms]
    pws = [_dot(m, m) for m in ms]
    n = 2
    while 2 * n < c:
        st = [_dot(jnp.concatenate([t, p], 0), p) for t, p in zip(ts, pws)]
        ts = [t + s[:c] for t, s in zip(ts, st)]
        pws = [s[c:] for s in st]
        n *= 2
    return [t + _dot(t, p) for t, p in zip(ts, pws)]


def _scan_chunk(hts, r_t, a_t, v, k_h, b_h, pc, n_rk, n_rb=None, m_ab=None, m_ak=None):
    zipm = lambda f, *ls: [f(*xs) for xs in zip(*ls)]
    if m_ab is None:
        y0 = zipm(_dot, n_rk, v)
        s0 = zipm(_dot_tn, v, k_h)
        ys = zipm(lambda r, ht, y: _dot_nt(r, ht) + y, r_t, hts, y0)
        return ys, zipm(lambda ht, p, s: ht * p + s, hts, pc, s0)
    dk = a_t[0].shape[1]
    ts = _unit_inverses(m_ab)
    mv = zipm(_dot, m_ak, v)
    wu = zipm(lambda t, a, m: _dot(t, jnp.concatenate([a, m], 1)), ts, a_t, mv)
    nwu = zipm(_dot, n_rb, wu)
    nkv = zipm(_dot, n_rk, v)
    gs = zipm(_dot_tn, wu, b_h)
    vk = zipm(_dot_tn, v, k_h)
    ys = zipm(lambda r, n, ht, y: _dot_nt(r + n[:, :dk], ht) + n[:, dk:] + y, r_t, nwu, hts, nkv)
    return ys, zipm(lambda ht, p, g_, s: ht * p + _dot(ht, g_[:dk]) + g_[dk:] + s, hts, pc, gs, vk)


def _time_block(d, j, ncc, nct):
    bwd = jnp.where(j < ncc, ncc - 1 - j, nct - 1 - j + ncc)
    return jnp.where(d == 0, j, bwd)


def _cum_scalars(col_g, row_g, incl):
    tri = incl.astype(F32)
    cum_col = _dot(tri, col_g, HIGHEST)
    cum_row = _dot_nt(row_g, tri, HIGHEST)
    tot = jnp.sum(col_g, 0, keepdims=True)
    return cum_col, cum_row, tot


def _dn_chunk_args(q_ref, k_ref, v_ref, col_ref, row_ref, d, heads, dh):
    c = q_ref.shape[1]
    incl, strict = _order_masks(c, d)
    col = col_ref[0, 0]
    row = row_ref[0, 0, 0]
    cum_col, cum_row, tot = _cum_scalars(col[:, heads:], row[heads:], incl)
    hs = range(heads)
    sls = [slice(h * dh, (h + 1) * dh) for h in hs]
    qs, ks, vs = ([ref[0, :, sl] for sl in sls] for ref in (q_ref, k_ref, v_ref))
    kks = [_dot_nt(k, k) for k in ks]
    qks = [_dot_nt(q, k) for q, k in zip(qs, ks)]
    args = [[] for _ in range(10)]
    for h in hs:
        q, k = qs[h], ks[h]
        bc, gc = col[:, h:h + 1], col[:, heads + h:heads + h + 1]
        br, gr = row[h:h + 1], row[heads + h:heads + h + 1]
        cc, cr, tt = cum_col[:, h:h + 1], cum_row[h:h + 1], tot[:, h:h + 1]
        kkb = kks[h] * br
        qkb = qks[h] * br
        dif = cc - cr
        m_ak = kkb * jnp.exp(jnp.where(strict, dif - gc, NEG_BIG))
        m_ab = -kkb * jnp.exp(jnp.where(strict, dif - gc + gr, NEG_BIG))
        n_rk = qkb * jnp.exp(jnp.where(incl, dif, NEG_BIG))
        n_rb = -qkb * jnp.exp(jnp.where(incl, dif + gr, NEG_BIG))
        a_t = k * jnp.exp(cc - gc)
        r_t = q * jnp.exp(cc)
        k_h = k * (bc * jnp.exp(tt - cc))
        b_h = -k * (bc * jnp.exp(gc + tt - cc))
        for lst, val in zip(args, (r_t, a_t, vs[h], k_h, b_h, jnp.exp(tt), n_rk, n_rb, m_ab, m_ak)):
            lst.append(val)
    return args


def _dn_scan_kernel(qf_ref, kf_ref, vf_ref, colf_ref, rowf_ref, qb_ref, kb_ref, vb_ref, colb_ref, rowb_ref,
                    of_ref, ob_ref, st_ref, *, heads, dh):
    @pl.when(pl.program_id(1) == 0)
    def _():
        st_ref[...] = jnp.zeros_like(st_ref)

    fwd = _dn_chunk_args(qf_ref, kf_ref, vf_ref, colf_ref, rowf_ref, 0, heads, dh)
    bwd = _dn_chunk_args(qb_ref, kb_ref, vb_ref, colb_ref, rowb_ref, 1, heads, dh)
    ys, hts = _scan_chunk([st_ref[h] for h in range(2 * heads)], *[f + b for f, b in zip(fwd, bwd)])
    for h in range(2 * heads):
        st_ref[h] = hts[h]
    for h in range(heads):
        of_ref[0, :, h * dh:(h + 1) * dh] = ys[h]
        ob_ref[0, :, h * dh:(h + 1) * dh] = ys[heads + h]


def dn_scan(q, k, v, beta, g, n_ctx, col_blocks=(0, 0, 0)):
    bn, l_len, _ = q.shape
    heads = beta.shape[-1]
    dh = DN_HEAD_DIM
    w = heads * dh
    c = SCAN_CHUNK
    nct, ncc = l_len // c, n_ctx // c
    col = jnp.concatenate([jnp.broadcast_to(beta[None], g.shape), g], -1)
    row = jnp.swapaxes(col.reshape(2, bn, nct, c, 2 * heads), 3, 4)
    tb = lambda d, j: _time_block(d, j, ncc, nct)
    seq = lambda d, cb: pl.BlockSpec((1, c, w), lambda b, j: (b, tb(d, j), cb))
    side = lambda d: [seq(d, col_blocks[0]), seq(d, col_blocks[1]), seq(d, col_blocks[2]),
                      pl.BlockSpec((1, 1, c, 2 * heads), lambda b, j: (d, b, tb(d, j), 0)),
                      pl.BlockSpec((1, 1, 1, 2 * heads, c), lambda b, j: (d, b, tb(d, j), 0, 0))]
    out = jax.ShapeDtypeStruct((bn, l_len, w), F32)
    return pl.pallas_call(
        functools.partial(_dn_scan_kernel, heads=heads, dh=dh),
        out_shape=[out, out],
        grid=(bn, nct),
        in_specs=side(0) + side(1),
        out_specs=[pl.BlockSpec((1, c, w), lambda b, j: (b, tb(0, j), 0)),
                   pl.BlockSpec((1, c, w), lambda b, j: (b, tb(1, j), 0))],
        scratch_shapes=[pltpu.VMEM((2 * heads, dh, dh), F32)],
        compiler_params=pltpu.CompilerParams(dimension_semantics=("parallel", "arbitrary")),
        name="dn_scan",
    )(q, k, v, col, row, q, k, v, col, row)


def _ssd_scan_kernel(c_ref, b_ref, x_ref, col_ref, row_ref, o_ref, st_ref, *, heads, groups, dh, n_st):
    d = pl.program_id(1)

    @pl.when(pl.program_id(2) == 0)
    def _():
        st_ref[...] = jnp.zeros_like(st_ref)

    c = c_ref.shape[1]
    incl, _ = _order_masks(c, d)
    col = col_ref[0, 0]
    row = row_ref[0, 0, 0]
    cum_col, cum_row, tot = _cum_scalars(col[:, heads:], row, incl)
    hg = heads // groups
    gw = hg * dh
    spread = (lax.broadcasted_iota(jnp.int32, (hg, gw), 1) // dh
              == lax.broadcasted_iota(jnp.int32, (hg, gw), 0)).astype(F32)
    for g in range(groups):
        hsl = slice(g * hg, (g + 1) * hg)
        cm = c_ref[0, :, g * n_st:(g + 1) * n_st]
        bm = b_ref[0, :, g * n_st:(g + 1) * n_st]
        cb = _dot_nt(cm, bm)
        cum_g = cum_col[:, hsl]
        dt_x = _dot_sel(col[:, hsl], spread)
        ecc_x = _dot_sel(jnp.exp(cum_g), spread)
        eend_x = _dot_sel(jnp.exp(tot[:, hsl] - cum_g), spread)
        v = x_ref[0, :, g * gw:(g + 1) * gw] * dt_x
        hst = st_ref[g]
        y_off = _dot_nt(cm, hst) * ecc_x
        upd = _dot_tn(v * eend_x, bm)
        for h in range(hg):
            ha = g * hg + h
            sl = slice(h * dh, (h + 1) * dh)
            n_rk = cb * jnp.exp(jnp.where(incl, cum_col[:, ha:ha + 1] - cum_row[ha:ha + 1], NEG_BIG))
            o_ref[0, 0, :, g * gw + h * dh:g * gw + (h + 1) * dh] = y_off[:, sl] + _dot(n_rk, v[:, sl])
            st_ref[g, sl, :] = hst[sl] * jnp.exp(tot[:, ha:ha + 1]) + upd[sl]


def ssd_scan(cm, bm, x, dt, da, n_ctx, col_blocks=(0, 0, 0)):
    bn, l_len, _ = x.shape
    heads = dt.shape[-1]
    dh, n_st, grp = SSD_HEAD_DIM, SSD_STATE, SSD_GROUPS
    cb0, bb0, xb0 = col_blocks
    c = SCAN_CHUNK
    nct, ncc = l_len // c, n_ctx // c
    col = jnp.concatenate([dt, da], -1)
    row = jnp.swapaxes(da.reshape(2, bn, nct, c, heads), 3, 4)
    tb = lambda d, j: _time_block(d, j, ncc, nct)
    gseq = lambda off: pl.BlockSpec((1, c, grp * n_st), lambda b, d, j: (b, tb(d, j), off))
    return pl.pallas_call(
        functools.partial(_ssd_scan_kernel, heads=heads, groups=grp, dh=dh, n_st=n_st),
        out_shape=jax.ShapeDtypeStruct((2, bn, l_len, heads * dh), F32),
        grid=(bn, 2, nct),
        in_specs=[gseq(cb0), gseq(bb0),
                  pl.BlockSpec((1, c, heads * dh), lambda b, d, j: (b, tb(d, j), xb0)),
                  pl.BlockSpec((1, 1, c, 2 * heads), lambda b, d, j: (d, b, tb(d, j), 0)),
                  pl.BlockSpec((1, 1, 1, heads, c), lambda b, d, j: (d, b, tb(d, j), 0, 0))],
        out_specs=pl.BlockSpec((1, 1, c, heads * dh), lambda b, d, j: (d, b, tb(d, j), 0)),
        scratch_shapes=[pltpu.VMEM((grp, heads // grp * dh, n_st), F32)],
        compiler_params=pltpu.CompilerParams(dimension_semantics=("parallel", "parallel", "arbitrary")),
        name="ssd_scan",
    )(cm, bm, x, col, row)


def _rwkv_scan_kernel(r_ref, k_ref, v_ref, kk_ref, b_ref, lw_ref, o_ref, st_ref, *, heads, dh):
    d = pl.program_id(2)

    @pl.when(pl.program_id(3) == 0)
    def _():
        st_ref[...] = jnp.zeros_like(st_ref)

    c = r_ref.shape[1]
    incl, strict = _order_masks(c, d)
    lw_all = lw_ref[0, 0]
    cl_all = _sel_dot(incl, lw_all)
    tot_all = jnp.sum(lw_all, 0, keepdims=True)
    hs = range(heads)
    sls = [slice(h * dh, (h + 1) * dh) for h in hs]
    args = [[] for _ in range(6)]
    grams = []
    for h in hs:
        sl = sls[h]
        r, k, v, kk, b = r_ref[0, :, sl], k_ref[0, :, sl], v_ref[0, :, sl], kk_ref[0, :, sl], b_ref[0, :, sl]
        lw, cl, tt = lw_all[:, sl], cl_all[:, sl], tot_all[:, sl]
        inv_p = jnp.exp(-cl)
        w_end = jnp.exp(tt - cl)
        a_t = -kk * jnp.exp(cl - lw)
        r_t = r * jnp.exp(cl)
        grams.append((a_t, r_t, b * inv_p, k * inv_p))
        for lst, val in zip(args, (r_t, a_t, v, k * w_end, b * w_end, jnp.exp(tt))):
            lst.append(val)
    gm = [_dot_nt(jnp.concatenate([a, r], 0), jnp.concatenate([b, k], 0)) for a, r, b, k in grams]
    m_ab = [jnp.where(strict, t[:c, :c], 0.0) for t in gm]
    m_ak = [jnp.where(strict, t[:c, c:], 0.0) for t in gm]
    n_rb = [jnp.where(incl, t[c:, :c], 0.0) for t in gm]
    n_rk = [jnp.where(incl, t[c:, c:], 0.0) for t in gm]
    ys, hts = _scan_chunk([st_ref[h] for h in hs], *args, n_rk, n_rb, m_ab, m_ak)
    for h in hs:
        st_ref[h] = hts[h]
        o_ref[0, 0, :, sls[h]] = ys[h]


def rwkv_scan(r, k, v, kk, b, lw, n_ctx, heads_per_step=32):
    bn, l_len, w = r.shape
    dh = RW_HEAD_DIM
    hb = heads_per_step
    c = SCAN_CHUNK
    nct, ncc = l_len // c, n_ctx // c
    tb = lambda d, j: _time_block(d, j, ncc, nct)
    seq = pl.BlockSpec((1, c, hb * dh), lambda b_, g, d, j: (b_, tb(d, j), g))
    return pl.pallas_call(
        functools.partial(_rwkv_scan_kernel, heads=hb, dh=dh),
        out_shape=jax.ShapeDtypeStruct((2, bn, l_len, w), F32),
        grid=(bn, w // (hb * dh), 2, nct),
        in_specs=[seq, seq, seq, seq, seq,
                  pl.BlockSpec((1, 1, c, hb * dh), lambda b_, g, d, j: (d, b_, tb(d, j), g))],
        out_specs=pl.BlockSpec((1, 1, c, hb * dh), lambda b_, g, d, j: (d, b_, tb(d, j), g)),
        scratch_shapes=[pltpu.VMEM((hb, dh, dh), F32)],
        compiler_params=pltpu.CompilerParams(
            dimension_semantics=("parallel", "parallel", "parallel", "arbitrary")),
        name="rwkv_scan",
    )(r, k, v, kk, b, lw)


def _router_kernel(h_ref, wr_ref, bias_ref, rank_ref, wt_ref, cnt_ref):
    tm = h_ref.shape[0]
    ne, ng = N_EXPERTS, N_GROUPS
    per = ne // ng
    h = h_ref[...]
    logits = sum(_dot_nt(piece, h) for piece in _split3(wr_ref[...]))
    scores = jax.nn.sigmoid(logits)
    sel = scores + bias_ref[...]
    sel3 = sel.reshape(ng, per, tm)
    idx3 = lax.broadcasted_iota(jnp.int32, (ng, per, tm), 1)
    m1 = jnp.max(sel3, 1, keepdims=True)
    first = jnp.min(jnp.where(sel3 == m1, idx3, per), 1, keepdims=True)
    m2 = jnp.max(jnp.where(idx3 == first, -jnp.inf, sel3), 1, keepdims=True)
    grp = (m1 + m2).reshape(ng, tm)
    gidx = lax.broadcasted_iota(jnp.int32, (ng, tm), 0)
    chosen = jnp.zeros((ng, tm), jnp.bool_)
    for _ in range(TOPK_GROUPS):
        gm = jnp.max(grp, 0, keepdims=True)
        gi = jnp.min(jnp.where(grp == gm, gidx, ng), 0, keepdims=True)
        hit = gidx == gi
        chosen = chosen | hit
        grp = jnp.where(hit, -jnp.inf, grp)
    selm = jnp.where(chosen.reshape(ng, 1, tm), sel3, -jnp.inf).reshape(ne, tm)
    eidx = lax.broadcasted_iota(jnp.int32, (ne, tm), 0)
    picked = jnp.zeros((ne, tm), jnp.bool_)
    for _ in range(TOP_K):
        em = jnp.max(selm, 0, keepdims=True)
        ei = jnp.min(jnp.where(selm == em, eidx, ne), 0, keepdims=True)
        hit = eidx == ei
        picked = picked | hit
        selm = jnp.where(hit, -jnp.inf, selm)
    wsel = jnp.where(picked, scores, 0.0)
    wt_ref[...] = wsel / jnp.sum(wsel, 0, keepdims=True) * ROUTED_SCALE
    onehot = picked.astype(BF16)
    before = (lax.broadcasted_iota(jnp.int32, (tm, tm), 0) < lax.broadcasted_iota(jnp.int32, (tm, tm), 1))
    rank = _dot(onehot, before.astype(BF16))
    rank_ref[...] = jnp.where(picked, rank, -1.0)
    cnt = jnp.sum(picked.astype(F32), 1, keepdims=True)
    cnt_ref[0] = jnp.broadcast_to(cnt, (ne, LANES)).astype(jnp.int32)


def _expert_kernel(cnt_ref, h_ref, rank_ref, wt_ref, wg_ref, wu_ref, wd_ref, sg_ref, su_ref, sd_ref, o_ref, *, sub):
    i, e = pl.program_id(0), pl.program_id(1)
    tm = h_ref.shape[0]
    n_sub = tm // sub
    mean = sub * TOP_K / N_EXPERTS
    rows = int(-(-(mean + 2.25 * mean ** 0.5) // 16) * 16)
    h = h_ref[...]

    d = o_ref.shape[1]
    cw = 512

    @pl.when(e == 0)
    def _():
        act = (jax.nn.silu(_dot(h, sg_ref[...])) * _dot(h, su_ref[...])).astype(BF16)
        for c0 in range(0, d, cw):
            o_ref[:, c0:c0 + cw] = _dot(act, sd_ref[:, c0:c0 + cw])

    rk = rank_ref[0]
    wt = wt_ref[0]
    c_max = cnt_ref[i * n_sub * N_EXPERTS + e]
    for s in range(1, n_sub):
        c_max = jnp.maximum(c_max, cnt_ref[(i * n_sub + s) * N_EXPERTS + e])
    n_blk = (c_max + rows - 1) // rows
    r_iota = lax.broadcasted_iota(jnp.int32, (rows, sub), 0).astype(F32)

    def body(b, carry):
        base = r_iota + (b * rows).astype(F32)
        hits = [rk[:, s * sub:(s + 1) * sub] == base for s in range(n_sub)]
        xg = jnp.concatenate([_dot(hit.astype(BF16), h[s * sub:(s + 1) * sub]) for s, hit in enumerate(hits)], 0)
        xg = xg.astype(BF16)
        act = jax.nn.silu(_dot(xg, wg_ref[0])) * _dot(xg, wu_ref[0])
        y = _dot(act.astype(BF16), wd_ref[0]).astype(BF16)
        for s, hit in enumerate(hits):
            gw = jnp.where(hit, wt[:, s * sub:(s + 1) * sub], 0.0).astype(BF16)
            for c0 in range(0, d, cw):
                o_ref[s * sub:(s + 1) * sub, c0:c0 + cw] += _dot_tn(gw, y[s * rows:(s + 1) * rows, c0:c0 + cw])
        return carry

    lax.fori_loop(0, n_blk, body, 0)


def moe(h, w_router, router_bias, we_gate, we_up, we_down, ws_gate, ws_up, ws_down, tm=1536, sub=384):
    n_tok, d = h.shape
    ne = N_EXPERTS
    nt = n_tok // tm
    ff = we_gate.shape[-1]
    rank, wt, cnt = pl.pallas_call(
        _router_kernel,
        out_shape=[jax.ShapeDtypeStruct((ne, n_tok), F32), jax.ShapeDtypeStruct((ne, n_tok), F32),
                   jax.ShapeDtypeStruct((n_tok // sub, ne, LANES), jnp.int32)],
        grid=(n_tok // sub,),
        in_specs=[pl.BlockSpec((sub, d), lambda i: (i, 0)), pl.BlockSpec((ne, d), lambda i: (0, 0)),
                  pl.BlockSpec((ne, 1), lambda i: (0, 0))],
        out_specs=[pl.BlockSpec((ne, sub), lambda i: (0, i)), pl.BlockSpec((ne, sub), lambda i: (0, i)),
                   pl.BlockSpec((1, ne, LANES), lambda i: (i, 0, 0))],
        compiler_params=pltpu.CompilerParams(dimension_semantics=("parallel",), vmem_limit_bytes=VMEM_LIMIT),
        name="moe_router",
    )(h, w_router.T.astype(F32), router_bias.astype(F32).reshape(ne, 1))
    counts = cnt[:, :, 0].reshape(-1)
    const = lambda i, e, c: (0, 0)
    once = pl.Buffered(1)
    return pl.pallas_call(
        functools.partial(_expert_kernel, sub=sub),
        out_shape=jax.ShapeDtypeStruct((n_tok, d), F32),
        grid_spec=pltpu.PrefetchScalarGridSpec(
            num_scalar_prefetch=1,
            grid=(nt, ne),
            in_specs=[pl.BlockSpec((tm, d), lambda i, e, c: (i, 0), pipeline_mode=once),
                      pl.BlockSpec((1, 1, tm), lambda i, e, c: (e, 0, i)),
                      pl.BlockSpec((1, 1, tm), lambda i, e, c: (e, 0, i)),
                      pl.BlockSpec((1, d, ff), lambda i, e, c: (e, 0, 0)),
                      pl.BlockSpec((1, d, ff), lambda i, e, c: (e, 0, 0)),
                      pl.BlockSpec((1, ff, d), lambda i, e, c: (e, 0, 0)),
                      pl.BlockSpec(ws_gate.shape, const, pipeline_mode=once),
                      pl.BlockSpec(ws_up.shape, const, pipeline_mode=once),
                      pl.BlockSpec(ws_down.shape, const, pipeline_mode=once)],
            out_specs=pl.BlockSpec((tm, d), lambda i, e, c: (i, 0), pipeline_mode=once)),
        compiler_params=pltpu.CompilerParams(dimension_semantics=("parallel", "arbitrary"),
                                             vmem_limit_bytes=VMEM_LIMIT),
        name="moe_experts",
    )(counts, h, rank.reshape(ne, 1, n_tok), wt.reshape(ne, 1, n_tok),
      we_gate.astype(BF16), we_up.astype(BF16), we_down.astype(BF16),
      ws_gate.astype(BF16), ws_up.astype(BF16), ws_down.astype(BF16))


def _segments(t, n_ctx, fn):
    return jnp.concatenate([fn(t[:, :n_ctx]), fn(t[:, n_ctx:])], 1)


def _dw_conv(t, w):
    pad = w.shape[0] // 2
    return lax.conv_general_dilated(
        t, w[:, None, :].astype(t.dtype), window_strides=(1,), padding=((pad, pad),),
        dimension_numbers=('NWC', 'WIO', 'NWC'), feature_group_count=t.shape[-1])


def _l2n(t):
    return t * lax.rsqrt(jnp.sum(jnp.square(t), -1, keepdims=True) + RMS_EPS)


def _heads(t, n):
    return t.reshape(*t.shape[:-1], n, t.shape[-1] // n)


def _pad_cols(w, width):
    return jnp.pad(w, ((0, 0), (0, width - w.shape[1])))


def _pad_rows(w, height):
    return jnp.pad(w, ((0, height - w.shape[0]), (0, 0)))


def _mod_vectors(c, c_ctx, w_mod, b_mod):
    bn, d = c.shape
    cc = jnp.concatenate([c, c_ctx[None], jnp.zeros((8 - bn - 1, d), F32)], 0)
    mv = matmul([(jax.nn.silu(cc), w_mod.astype(BF16))]) + b_mod
    mv = mv.reshape(8, 6, d)
    lat = mv[:bn]
    ctx = jnp.broadcast_to(mv[bn][None], lat.shape)
    return jnp.stack([ctx, lat], 1)


def _pack_mv(gate, shift, scale):
    z = jnp.zeros_like(gate)
    return jnp.stack([gate, shift, scale, z, z, z, z, z], 2)


def _cat_pad(parts):
    return jnp.concatenate([_pad_cols(a, wd) if a.ndim == 2 else jnp.pad(a, (0, wd - a.shape[0]))
                            for a, wd in parts], -1)


def _seg_allsum(x, seg):
    if seg == LANES:
        parts = [x[:, c0:c0 + LANES] for c0 in range(0, x.shape[-1], LANES)]
        parts = [jnp.broadcast_to(jnp.sum(t, -1, keepdims=True), t.shape) for t in parts]
        return parts[0] if len(parts) == 1 else jnp.concatenate(parts, -1)
    bw = 2 * LANES
    same = (lax.broadcasted_iota(jnp.int32, (bw, bw), 0) // seg) == (lax.broadcasted_iota(jnp.int32, (bw, bw), 1) // seg)
    ones_bd = jnp.where(same, 1.0, 0.0).astype(BF16)
    hi, mid, lo = _split3(x)
    parts = []
    for c0 in range(0, x.shape[-1], bw):
        sl = slice(c0, c0 + bw)
        parts.append(_dot(hi[:, sl], ones_bd) + _dot(mid[:, sl], ones_bd) + _dot(lo[:, sl], ones_bd))
    return parts[0] if len(parts) == 1 else jnp.concatenate(parts, -1)


def _halo_rows(x, xp_ref, xn_ref, i, n_ctx_tiles):
    has_prev = (i > 0) & (i != n_ctx_tiles)
    has_next = (i != n_ctx_tiles - 1) & (i != pl.num_programs(1) - 1)
    xp = jnp.where(has_prev, xp_ref[0], 0.0)
    xn = jnp.where(has_next, xn_ref[0], 0.0)
    return jnp.concatenate([xp, x, xn], 0)


def _halo_specs(tr, width, l_len, col_block):
    nh, n8 = tr // 8, l_len // 8
    return [pl.BlockSpec((1, tr, width), lambda b, i, *s: (b, i, col_block(*s))),
            pl.BlockSpec((1, 8, width), lambda b, i, *s: (b, jnp.maximum(i * nh - 1, 0), col_block(*s))),
            pl.BlockSpec((1, 8, width), lambda b, i, *s: (b, jnp.minimum((i + 1) * nh, n8 - 1), col_block(*s)))]


def _conv_kernel(x_ref, xp_ref, xn_ref, w_ref, b_ref, o_ref, *, n_ctx_tiles, kw, l2_segments):
    i, s = pl.program_id(1), pl.program_id(2)
    x = x_ref[0]
    tr = x.shape[0]
    xe = _halo_rows(x, xp_ref, xn_ref, i, n_ctx_tiles)
    pad = kw // 2
    acc = jnp.broadcast_to(b_ref[...], x.shape)
    for j in range(kw):
        acc = acc + w_ref[j:j + 1, :] * xe[8 - pad + j:8 - pad + j + tr]
    y = acc * jax.nn.sigmoid(acc)
    if l2_segments:
        yn = y * lax.rsqrt(_seg_allsum(y * y, DN_HEAD_DIM) + RMS_EPS)
        yn = yn * jnp.where(s == 0, DN_HEAD_DIM ** -0.5, 1.0)
        y = jnp.where(s < 2, yn, y)
    o_ref[0] = y


def conv_silu(p, col0, width, seg_w, conv_w, conv_b, n_ctx, l2_segments=False):
    bn, l_len, _ = p.shape
    tr = 256
    cb = col0 // seg_w
    return pl.pallas_call(
        functools.partial(_conv_kernel, n_ctx_tiles=n_ctx // tr, kw=conv_w.shape[0], l2_segments=l2_segments),
        out_shape=jax.ShapeDtypeStruct((bn, l_len, width), F32),
        grid=(bn, l_len // tr, width // seg_w),
        in_specs=_halo_specs(tr, seg_w, l_len, lambda s: cb + s)
        + [pl.BlockSpec((8, seg_w), lambda b, i, s: (0, s)), pl.BlockSpec((1, seg_w), lambda b, i, s: (0, s))],
        out_specs=pl.BlockSpec((1, tr, seg_w), lambda b, i, s: (b, i, s)),
        compiler_params=pltpu.CompilerParams(dimension_semantics=("parallel", "parallel", "parallel"),
                                             vmem_limit_bytes=VMEM_LIMIT),
        name="conv_silu",
    )(p, p, p, _pad_rows(conv_w.astype(F32), 8), conv_b.astype(F32).reshape(1, width))


def _softplus(x):
    return jnp.maximum(x, 0.0) + jnp.log1p(jnp.exp(-jnp.abs(x)))


def _rwkv_prep_kernel(x_ref, xp_ref, xn_ref, mu_ref, vec_ref, a2_ref, g2_ref, w2f_ref, w2b_ref,
                      r_ref, k_ref, v_ref, kk_ref, b_ref, lw_ref, g_ref, *, n_ctx_tiles):
    x = x_ref[0]
    tr = x.shape[0]
    xe = _halo_rows(x, xp_ref, xn_ref, pl.program_id(1), n_ctx_tiles)
    x = x + mu_ref[...] * (0.5 * (xe[7:7 + tr] + xe[9:9 + tr]) - x)
    w = RW_W
    r, k, v = x[:, :w], x[:, w:2 * w], x[:, 2 * w:3 * w]
    o = 3 * w
    wl_f, wl_b, al = x[:, o:o + LANES], x[:, o + LANES:o + 2 * LANES], x[:, o + 2 * LANES:o + 3 * LANES]
    gl = x[:, o + 3 * LANES:o + 3 * LANES + RW_GATE_LORA]
    vec = vec_ref[...]
    a = jax.nn.sigmoid(vec[0:1] + _dot(al, a2_ref[...]))
    g_ref[0] = _dot(jax.nn.sigmoid(gl), g2_ref[...])
    kk = k * vec[1:2]
    kk = kk * lax.rsqrt(_seg_allsum(kk * kk, RW_HEAD_DIM) + RMS_EPS)
    r_ref[0] = r
    v_ref[0] = v
    k_ref[0] = k * (1.0 + (a - 1.0) * vec[2:3])
    kk_ref[0] = kk
    b_ref[0] = kk * a
    for di, (wl, w2_ref) in enumerate(((wl_f, w2f_ref), (wl_b, w2b_ref))):
        w_log = -_softplus(-(vec[3 + di:4 + di] + _dot(jnp.tanh(wl), w2_ref[...]))) - 0.5
        lw_ref[di, 0] = -jnp.exp(w_log)


def rwkv_prep(p_rw, mu, vec, a2, g2, w2f, w2b, n_ctx):
    bn, l_len, wp = p_rw.shape
    tr = 128
    w = RW_W
    full = lambda a: pl.BlockSpec(a.shape, lambda b, i: (0,) * a.ndim)
    seq = pl.BlockSpec((1, tr, w), lambda b, i: (b, i, 0))
    sds = jax.ShapeDtypeStruct((bn, l_len, w), F32)
    return pl.pallas_call(
        functools.partial(_rwkv_prep_kernel, n_ctx_tiles=n_ctx // tr),
        out_shape=[sds, sds, sds, sds, sds, jax.ShapeDtypeStruct((2, bn, l_len, w), F32), sds],
        grid=(bn, l_len // tr),
        in_specs=_halo_specs(tr, wp, l_len, lambda: 0) + [full(mu), full(vec), full(a2), full(g2), full(w2f), full(w2b)],
        out_specs=[seq, seq, seq, seq, seq, pl.BlockSpec((2, 1, tr, w), lambda b, i: (0, b, i, 0)), seq],
        compiler_params=pltpu.CompilerParams(dimension_semantics=("parallel", "parallel"),
                                             vmem_limit_bytes=VMEM_LIMIT),
        name="rwkv_prep",
    )(p_rw, p_rw, p_rw, mu, vec, a2, g2, w2f, w2b)


def _l0_out_kernel(att_ref, of_ref, ob_ref, zg_ref, nw_ref, w_ref, z_ref, mv_ref, g_ref, b_ref, zn_ref, h_ref):
    o = of_ref[0] + ob_ref[0]
    o = o * lax.rsqrt(_seg_allsum(o * o, DN_HEAD_DIM) * (1.0 / DN_HEAD_DIM) + RMS_EPS) * nw_ref[...]
    zg = zg_ref[0]
    dn = o * (zg * jax.nn.sigmoid(zg))
    mix = _dot(att_ref[0], w_ref[:A_Q]) + _dot(dn, w_ref[A_Q:])
    _residual_ln_mod(z_ref[0], mix, mv_ref[0, 0], g_ref[...], b_ref[...], zn_ref, h_ref)


def _l1_out_kernel(yf_ref, yb_ref, xs_ref, zs_ref, rf_ref, rb_ref, r_ref, k_ref, v_ref, gt_ref, vec_ref, w_ref,
                   z_ref, mv_ref, g_ref, b_ref, zn_ref, h_ref):
    vec = vec_ref[...]
    zs = zs_ref[0]
    y = (yf_ref[0, 0] + yb_ref[0, 0] + vec[0:1] * xs_ref[0]) * (zs * jax.nn.sigmoid(zs))
    gw = SSD_W // SSD_GROUPS
    parts = []
    for gi in range(SSD_GROUPS):
        yg = y[:, gi * gw:(gi + 1) * gw]
        parts.append(yg * lax.rsqrt(jnp.mean(yg * yg, -1, keepdims=True) + RMS_EPS))
    ssd = jnp.concatenate(parts, -1) * vec[1:2]
    yr = rf_ref[0, 0] + rb_ref[0, 0]
    inv = 1.0 / RW_HEAD_DIM
    yc = yr - _seg_allsum(yr, RW_HEAD_DIM) * inv
    var = _seg_allsum(yc * yc, RW_HEAD_DIM) * inv
    yr = yc * lax.rsqrt(var + RW_LN_EPS) * vec[2:3] + vec[3:4]
    yr = yr + _seg_allsum(r_ref[0] * k_ref[0] * vec[4:5], RW_HEAD_DIM) * v_ref[0]
    rw = yr * gt_ref[0]
    mix = _dot(ssd, w_ref[:SSD_W]) + _dot(rw, w_ref[SSD_W:])
    _residual_ln_mod(z_ref[0], mix, mv_ref[0, 0], g_ref[...], b_ref[...], zn_ref, h_ref)


def _mixer_out_call(body, name, tr, seq_inputs, vec_inputs, w_out, z, mv, ln_g, ln_b, n_ctx):
    bn, l_len, d = z.shape
    nbc = n_ctx // tr
    full = lambda a: pl.BlockSpec(a.shape, lambda b, i: (0,) * a.ndim)
    row = pl.BlockSpec((1, tr, d), lambda b, i: (b, i, 0))
    w_b = w_out.astype(BF16)
    return pl.pallas_call(
        body,
        out_shape=[jax.ShapeDtypeStruct((bn, l_len, d), F32), jax.ShapeDtypeStruct((bn, l_len, d), BF16)],
        grid=(bn, l_len // tr),
        in_specs=[pl.BlockSpec(bs, im) for _, bs, im in seq_inputs] + [full(a) for a in vec_inputs]
        + [pl.BlockSpec(w_b.shape, lambda b, i: (0, 0), pipeline_mode=pl.Buffered(1)), row,
           pl.BlockSpec((1, 1, 8, d), lambda b, i: (b, jnp.where(i < nbc, 0, 1), 0, 0)),
           pl.BlockSpec((1, d), lambda b, i: (0, 0)), pl.BlockSpec((1, d), lambda b, i: (0, 0))],
        out_specs=[row, row],
        compiler_params=pltpu.CompilerParams(dimension_semantics=("parallel", "parallel"),
                                             vmem_limit_bytes=VMEM_LIMIT),
        name=name,
    )(*[a for a, _, _ in seq_inputs], *vec_inputs, w_b, z, mv, ln_g.reshape(1, d), ln_b.reshape(1, d))


def _layer0(z, h, mv, n_ctx, w_in, w_out, attn_sink, conv_w, a_log, dt_bias, norm_w, ln_g, ln_b, cs):
    bn, l_len, d = h.shape
    q, k, v, dq, dk, dv, zz, sm = jnp.split(
        w_in, np.cumsum([A_Q, A_KV, A_KV, DN_W, DN_W, DN_W, DN_W]).tolist(), 1)
    w_in_p = jnp.concatenate([q, dq, dk, dv, zz, k, v, _pad_cols(sm, LANES)], 1).astype(BF16)
    k_col = A_Q + 4 * DN_W
    p = matmul([(h.reshape(bn * l_len, d), w_in_p)]).reshape(bn, l_len, -1)
    att = attention(p, cs, attn_sink, n_ctx, 0, k_col, k_col + A_KV)
    qkv = conv_silu(p, A_Q, 3 * DN_W, DN_W, conv_w, jnp.zeros((3 * DN_W,), F32), n_ctx, l2_segments=True)
    sm_col = k_col + 2 * A_KV
    sm = p[..., sm_col:sm_col + 3 * DN_HEADS]
    a_f, a_b, bb = sm[..., :DN_HEADS], sm[..., DN_HEADS:2 * DN_HEADS], sm[..., 2 * DN_HEADS:]
    beta = jax.nn.sigmoid(bb)
    g = jnp.stack([-jnp.exp(a_log[0]) * jax.nn.softplus(a_f + dt_bias[0]),
                   -jnp.exp(a_log[1]) * jax.nn.softplus(a_b + dt_bias[1])], 0)
    o = dn_scan(qkv, qkv, qkv, beta, g, n_ctx, col_blocks=(0, 1, 2))
    tr = 256
    seq = [(att, (1, tr, A_Q), lambda b, i: (b, i, 0)),
           (o[0], (1, tr, DN_W), lambda b, i: (b, i, 0)),
           (o[1], (1, tr, DN_W), lambda b, i: (b, i, 0)),
           (p, (1, tr, DN_W), lambda b, i: (b, i, (A_Q + 3 * DN_W) // DN_W))]
    return _mixer_out_call(_l0_out_kernel, "l0_out", tr, seq, [jnp.tile(norm_w, DN_HEADS).reshape(1, DN_W)],
                           w_out, z, mv, ln_g, ln_b, n_ctx)


def _layer1(z, h, mv, n_ctx, w_in, w_out, ssd_conv_w, ssd_conv_b, ssd_a_log, ssd_dt_bias, ssd_d, ssd_norm_w,
            rw_mu, rw_w0, rw_w2, rw_a0, rw_a2, rw_g2, rw_k_k, rw_k_a, rw_r_k, rw_ln_g, rw_ln_b, ln_g, ln_b):
    bn, l_len, d = h.shape
    n_tok = bn * l_len
    ssd_in = 2 * SSD_W + 2 * SSD_GN + 2 * SSD_HEADS
    lora_w = [RW_DECAY_LORA, RW_DECAY_LORA, RW_A_LORA]
    h2 = h.reshape(n_tok, d)
    w_ssd = _pad_cols(w_in[:, :ssd_in], 5120).astype(BF16)
    p_ssd = matmul([(h2, w_ssd)]).reshape(bn, l_len, -1)
    xbc = conv_silu(p_ssd, SSD_W, SSD_W + 2 * SSD_GN, 512, ssd_conv_w, ssd_conv_b, n_ctx)
    dtc = p_ssd[..., 2 * SSD_W + 2 * SSD_GN:ssd_in]
    dt = jnp.stack([jax.nn.softplus(dtc[..., :SSD_HEADS] + ssd_dt_bias[0]),
                    jax.nn.softplus(dtc[..., SSD_HEADS:] + ssd_dt_bias[1])], 0)
    da = -jnp.exp(ssd_a_log)[:, None, None, :] * dt
    y_ssd = ssd_scan(xbc, xbc, xbc, dt, da, n_ctx,
                     col_blocks=((SSD_W + SSD_GN) // SSD_GN, SSD_W // SSD_GN, 0))
    splits = np.cumsum([RW_W, RW_W, RW_W] + lora_w).tolist()
    widths = [RW_W, RW_W, RW_W, LANES, LANES, LANES, RW_GATE_LORA]
    w_rw = _cat_pad(list(zip(jnp.split(w_in[:, ssd_in:], splits, 1), widths)))
    w_rw = _pad_cols(w_rw, 6912).astype(BF16)
    mu = _cat_pad(list(zip(jnp.split(rw_mu, splits), widths)))
    mu = jnp.pad(mu, (0, w_rw.shape[1] - mu.shape[0])).reshape(1, -1)
    p_rw = matmul([(h2, w_rw)]).reshape(bn, l_len, -1)
    zero = jnp.zeros((RW_W,), F32)
    vec = jnp.stack([rw_a0, rw_k_k, rw_k_a, rw_w0[0], rw_w0[1], zero, zero, zero], 0)
    lora_p = lambda w: _pad_rows(w, LANES).astype(BF16)
    r, k, v, kk, b, lw, g = rwkv_prep(p_rw, mu, vec, lora_p(rw_a2), rw_g2.astype(BF16), lora_p(rw_w2[0]),
                                      lora_p(rw_w2[1]), n_ctx)
    y_rw = rwkv_scan(r, k, v, kk, b, lw, n_ctx)
    tr = 128
    rowmap = lambda b_, i: (b_, i, 0)
    seq = [(y_ssd, (1, 1, tr, SSD_W), lambda b_, i: (0, b_, i, 0)),
           (y_ssd, (1, 1, tr, SSD_W), lambda b_, i: (1, b_, i, 0)),
           (xbc, (1, tr, SSD_W), rowmap), (p_ssd, (1, tr, SSD_W), rowmap),
           (y_rw, (1, 1, tr, RW_W), lambda b_, i: (0, b_, i, 0)),
           (y_rw, (1, 1, tr, RW_W), lambda b_, i: (1, b_, i, 0)),
           (r, (1, tr, RW_W), rowmap), (k, (1, tr, RW_W), rowmap), (v, (1, tr, RW_W), rowmap),
           (g, (1, tr, RW_W), rowmap)]
    vec_out = jnp.stack([jnp.repeat(ssd_d, SSD_HEAD_DIM), ssd_norm_w, rw_ln_g, rw_ln_b, rw_r_k.reshape(-1),
                         zero, zero, zero], 0)
    return _mixer_out_call(_l1_out_kernel, "l1_out", tr, seq, [vec_out], w_out, z, mv, ln_g, ln_b, n_ctx)


def kernel(x, c, ctx, c_ctx, l0_w_mod, l0_b_mod, l0_ln1_g, l0_ln1_b, l0_ln2_g, l0_ln2_b, l0_w_in, l0_w_out, l0_attn_sink, l0_dn_conv_w, l0_dn_a_log, l0_dn_dt_bias, l0_dn_norm_w, l0_w_router, l0_router_bias, l0_we_gate, l0_we_up, l0_we_down, l0_ws_gate, l0_ws_up, l0_ws_down, l1_w_mod, l1_b_mod, l1_ln1_g, l1_ln1_b, l1_ln2_g, l1_ln2_b, l1_w_in, l1_w_out, l1_ssd_conv_w, l1_ssd_conv_b, l1_ssd_a_log, l1_ssd_dt_bias, l1_ssd_d, l1_ssd_norm_w, l1_rw_mu, l1_rw_w0, l1_rw_w2, l1_rw_a0, l1_rw_a2, l1_rw_g2, l1_rw_k_k, l1_rw_k_a, l1_rw_r_k, l1_rw_ln_g, l1_rw_ln_b, l1_w_router, l1_router_bias, l1_we_gate, l1_we_up, l1_we_down, l1_ws_gate, l1_ws_up, l1_ws_down):
    bn, s_len, d = x.shape
    n_ctx = ctx.shape[1]
    l_len = n_ctx + s_len
    n_tok = bn * l_len
    z = jnp.concatenate([ctx, x], 1)
    cs = rope_table(n_ctx, s_len)
    mv0 = _mod_vectors(c, c_ctx, l0_w_mod, l0_b_mod)
    mv1 = _mod_vectors(c, c_ctx, l1_w_mod, l1_b_mod)
    pick = lambda mv, gi, shi, sci: _pack_mv(mv[:, :, gi], mv[:, :, shi], mv[:, :, sci])

    (h,) = ln_mod(z, z, pick(mv0, 2, 0, 1), l0_ln1_g, l0_ln1_b, n_ctx, do_ln=False, do_mod=True)
    z, h = _layer0(z, h, pick(mv0, 2, 3, 4), n_ctx, l0_w_in, l0_w_out, l0_attn_sink, l0_dn_conv_w, l0_dn_a_log,
                   l0_dn_dt_bias, l0_dn_norm_w, l0_ln1_g, l0_ln1_b, cs)
    f = moe(h.reshape(n_tok, d), l0_w_router, l0_router_bias, l0_we_gate, l0_we_up, l0_we_down,
            l0_ws_gate, l0_ws_up, l0_ws_down).reshape(bn, l_len, d)
    mv_a = _pack_mv(mv0[:, :, 5], mv1[:, :, 0], mv1[:, :, 1])
    z, h = ln_mod(z, f, mv_a, l0_ln2_g, l0_ln2_b, n_ctx, do_ln=True, do_mod=True)

    z, h = _layer1(z, h, pick(mv1, 2, 3, 4), n_ctx, l1_w_in, l1_w_out, l1_ssd_conv_w, l1_ssd_conv_b, l1_ssd_a_log,
                   l1_ssd_dt_bias, l1_ssd_d, l1_ssd_norm_w, l1_rw_mu, l1_rw_w0, l1_rw_w2, l1_rw_a0, l1_rw_a2,
                   l1_rw_g2, l1_rw_k_k, l1_rw_k_a, l1_rw_r_k, l1_rw_ln_g, l1_rw_ln_b, l1_ln1_g, l1_ln1_b)
    f = moe(h.reshape(n_tok, d), l1_w_router, l1_router_bias, l1_we_gate, l1_we_up, l1_we_down,
            l1_ws_gate, l1_ws_up, l1_ws_down).reshape(bn, l_len, d)
    (zx,) = ln_mod(z, f, pick(mv1, 5, 3, 4), l1_ln2_g, l1_ln2_b, n_ctx, do_ln=True, do_mod=False, lat_only=True)
    return zx
```

```python
import functools

import jax
import jax.numpy as jnp
import numpy as np
from jax import lax
from jax.experimental import pallas as pl
from jax.experimental.pallas import tpu as pltpu

F32 = jnp.float32
BF16 = jnp.bfloat16

DEPTH = 2
GRID_W = 64
HEAD_DIM = 128
A_Q_HEADS = 8
A_KV_HEADS = 2
WINDOW = 128
ATT_BLOCK = 128
ROPE_THETA = 10000.0
DN_HEADS = 8
DN_HEAD_DIM = 128
SSD_HEADS = 32
SSD_HEAD_DIM = 64
SSD_GROUPS = 2
SSD_STATE = 128
RW_HEADS = 32
RW_HEAD_DIM = 64
RW_DECAY_LORA = 96
RW_A_LORA = 96
RW_GATE_LORA = 256
RW_LN_EPS = 64e-5
N_EXPERTS = 64
TOP_K = 8
N_GROUPS = 8
TOPK_GROUPS = 4
ROUTED_SCALE = 2.5
LN_EPS = 1e-5
RMS_EPS = 1e-6
DEEPNORM_ALPHA = (2 * DEPTH) ** 0.25

A_Q = A_Q_HEADS * HEAD_DIM
A_KV = A_KV_HEADS * HEAD_DIM
DN_W = DN_HEADS * DN_HEAD_DIM
SSD_W = SSD_HEADS * SSD_HEAD_DIM
SSD_GN = SSD_GROUPS * SSD_STATE
RW_W = RW_HEADS * RW_HEAD_DIM

LANES = 128
SCAN_CHUNK = 64
VMEM_LIMIT = 56 * 1024 * 1024
NEG_BIG = -1e30

HIGHEST = lax.Precision.HIGHEST


def _dot_dims(a, b, dims, precision):
    if precision is None:
        a, b = a.astype(BF16), b.astype(BF16)
    return lax.dot_general(a, b, (dims, ((), ())), preferred_element_type=F32, precision=precision)


def _dot(a, b, precision=None):
    return _dot_dims(a, b, ((1,), (0,)), precision)


def _dot_nt(a, b, precision=None):
    return _dot_dims(a, b, ((1,), (1,)), precision)


def _dot_tn(a, b, precision=None):
    return _dot_dims(a, b, ((0,), (0,)), precision)


def _split3(x):
    hi = x.astype(BF16)
    r1 = x - hi.astype(F32)
    mid = r1.astype(BF16)
    return hi, mid, (r1 - mid.astype(F32)).astype(BF16)


def _dot_sel(x, sel01):
    s = sel01.astype(BF16)
    hi, mid, lo = _split3(x)
    return _dot(hi, s) + _dot(mid, s) + _dot(lo, s)


def _sel_dot(sel01, x):
    s = sel01.astype(BF16)
    hi, mid, lo = _split3(x)
    return _dot(s, hi) + _dot(s, mid) + _dot(s, lo)


def _pick_tile(n, candidates):
    for c in candidates:
        if n % c == 0:
            return c
    return n


def _mm_kernel(*refs, n_pairs):
    o_ref = refs[2 * n_pairs]
    acc = None
    for a_ref, w_ref in zip(refs[:n_pairs], refs[n_pairs:2 * n_pairs]):
        p = _dot(a_ref[...].astype(BF16), w_ref[...].astype(BF16))
        acc = p if acc is None else acc + p
    o_ref[...] = acc.astype(o_ref.dtype)


def matmul(pairs, out_dtype=F32):
    m = pairs[0][0].shape[0]
    n = pairs[0][1].shape[1]
    tn = _pick_tile(n, (1280, 1152, 1024, 768, 512, 384, 256, 128))
    tm = _pick_tile(m, (1024, 512, 256, 128) if tn <= 768 else (512, 256, 128))
    in_specs = ([pl.BlockSpec((tm, a.shape[1]), lambda j, i: (i, 0)) for a, _ in pairs]
                + [pl.BlockSpec((w.shape[0], tn), lambda j, i: (0, j)) for _, w in pairs])
    return pl.pallas_call(
        functools.partial(_mm_kernel, n_pairs=len(pairs)),
        out_shape=jax.ShapeDtypeStruct((m, n), out_dtype),
        grid=(n // tn, m // tm),
        in_specs=in_specs,
        out_specs=pl.BlockSpec((tm, tn), lambda j, i: (i, j)),
        compiler_params=pltpu.CompilerParams(dimension_semantics=("parallel", "parallel"),
                                             vmem_limit_bytes=VMEM_LIMIT),
        name="matmul",
    )(*[a for a, _ in pairs], *[w for _, w in pairs])


def _residual_ln_mod(z, o, mv, g, b, zn_ref, h_ref):
    t = DEEPNORM_ALPHA * z + mv[0:1] * o
    tc = t - jnp.mean(t, -1, keepdims=True)
    zn = tc * lax.rsqrt(jnp.mean(tc * tc, -1, keepdims=True) + LN_EPS) * g + b
    zn_ref[0] = zn
    h_ref[0] = (zn * (1.0 + mv[2:3]) + mv[1:2]).astype(h_ref.dtype)


def _ln_mod_kernel(z_ref, o_ref, mv_ref, g_ref, b_ref, zn_ref, h_ref, *, do_ln, do_mod):
    z = z_ref[0]
    mv = mv_ref[0, 0]
    if do_ln:
        t = DEEPNORM_ALPHA * z + mv[0:1] * o_ref[0]
        mu = jnp.mean(t, -1, keepdims=True)
        tc = t - mu
        var = jnp.mean(tc * tc, -1, keepdims=True)
        z = tc * lax.rsqrt(var + LN_EPS) * g_ref[...] + b_ref[...]
        zn_ref[0] = z
    if do_mod:
        h_ref[0] = (z * (1.0 + mv[2:3]) + mv[1:2]).astype(h_ref.dtype)


def ln_mod(z, o, mv, ln_g, ln_b, n_ctx, *, do_ln, do_mod, lat_only=False):
    bn, l_len, d = z.shape
    tr = 256
    nbc = n_ctx // tr
    off = nbc if lat_only else 0
    n_out = l_len - off * tr
    row = lambda b, i: (b, i + off, 0)
    outs, out_specs = [], []
    if do_ln:
        outs.append(jax.ShapeDtypeStruct((bn, n_out, d), F32))
        out_specs.append(pl.BlockSpec((1, tr, d), lambda b, i: (b, i, 0)))
    if do_mod:
        outs.append(jax.ShapeDtypeStruct((bn, n_out, d), BF16))
        out_specs.append(pl.BlockSpec((1, tr, d), lambda b, i: (b, i, 0)))

    def body(z_ref, o_ref, mv_ref, g_ref, b_ref, *out_refs):
        zn_ref = out_refs[0] if do_ln else None
        h_ref = out_refs[-1] if do_mod else None
        _ln_mod_kernel(z_ref, o_ref, mv_ref, g_ref, b_ref, zn_ref, h_ref, do_ln=do_ln, do_mod=do_mod)

    res = pl.pallas_call(
        body,
        out_shape=outs,
        grid=(bn, n_out // tr),
        in_specs=[pl.BlockSpec((1, tr, d), row), pl.BlockSpec((1, tr, d), row),
                  pl.BlockSpec((1, 1, 8, d), lambda b, i: (b, jnp.where(i + off < nbc, 0, 1), 0, 0)),
                  pl.BlockSpec((1, d), lambda b, i: (0, 0)), pl.BlockSpec((1, d), lambda b, i: (0, 0))],
        out_specs=out_specs,
        compiler_params=pltpu.CompilerParams(dimension_semantics=("parallel", "parallel"),
                                             vmem_limit_bytes=VMEM_LIMIT),
        name="ln_mod",
    )(z, o, mv, ln_g.reshape(1, d), ln_b.reshape(1, d))
    return res


def _rope(x, cs):
    lane = lax.broadcasted_iota(jnp.int32, x.shape, 1)
    quarter = HEAD_DIM // 4
    first = (lane % (2 * quarter)) < quarter
    partner = jnp.where(first, pltpu.roll(x, HEAD_DIM - quarter, 1), pltpu.roll(x, quarter, 1))
    return x * cs[:, :HEAD_DIM] + partner * cs[:, HEAD_DIM:]


def _attn_kernel(q_ref, kp_ref, kc_ref, kn_ref, vp_ref, vc_ref, vn_ref, kx_ref, vx_ref,
                 csp_ref, csc_ref, csn_ref, sink_ref, o_ref, *, n_ctx_blocks, s_len):
    i = pl.program_id(1)
    blk = ATT_BLOCK
    grp = A_Q_HEADS // A_KV_HEADS
    n_ctx = kx_ref.shape[1]
    csc = csc_ref[...]
    q = q_ref[0]
    q4 = jnp.concatenate([_rope(q[:, h * HEAD_DIM:(h + 1) * HEAD_DIM], csc) for h in range(grp)], 0)
    k_all = jnp.concatenate([_rope(kp_ref[0], csp_ref[...]), _rope(kc_ref[0], csc),
                             _rope(kn_ref[0], csn_ref[...]), kx_ref[0]], 0)
    v_all = jnp.concatenate([vp_ref[0], vc_ref[0], vn_ref[0], vx_ref[0]], 0)
    n_keys = 3 * blk + n_ctx
    s = _dot_nt(q4, k_all) * HEAD_DIM ** -0.5
    s = s.reshape(grp, blk, n_keys)
    tq = lax.broadcasted_iota(jnp.int32, (1, blk, n_keys), 1)
    col = lax.broadcasted_iota(jnp.int32, (1, blk, n_keys), 2)
    koff = col - blk
    pos = (i - n_ctx_blocks) * blk + koff
    local_ok = (jnp.abs(tq - koff) <= WINDOW) & (pos >= 0) & (pos < s_len) & (i >= n_ctx_blocks)
    valid = local_ok | (col >= 3 * blk)
    s = jnp.where(valid, s, -jnp.inf)
    sink = sink_ref[0]
    hsel = lax.broadcasted_iota(jnp.int32, (grp, 1, 1), 0)
    sink3 = jnp.zeros((grp, 1, 1), F32)
    for h in range(grp):
        sink3 = jnp.where(hsel == h, sink[:, h:h + 1].reshape(1, 1, 1), sink3)
    m = jnp.maximum(jnp.max(s, -1, keepdims=True), sink3)
    p = jnp.exp(s - m)
    denom = jnp.sum(p, -1, keepdims=True) + jnp.exp(sink3 - m)
    o = _dot(p.reshape(grp * blk, n_keys), v_all).reshape(grp, blk, HEAD_DIM) / denom
    for h in range(grp):
        o_ref[0, :, h * HEAD_DIM:(h + 1) * HEAD_DIM] = o[h]


def attention(p_all, cs, sink, n_ctx, q_col, k_col, v_col):
    bn, l_len, _ = p_all.shape
    blk = ATT_BLOCK
    grp = A_Q_HEADS // A_KV_HEADS
    nb = l_len // blk
    ncb = n_ctx // blk
    s_len = l_len - n_ctx
    qw = grp * HEAD_DIM
    prev = lambda i: jnp.clip(i - 1, ncb, nb - 1)
    cur = lambda i: jnp.clip(i, ncb, nb - 1)
    nxt = lambda i: jnp.clip(i + 1, ncb, nb - 1)
    kb, vb = k_col // HEAD_DIM, v_col // HEAD_DIM

    def kv_spec(colb, rowf):
        return pl.BlockSpec((1, blk, HEAD_DIM), lambda b, i, h: (b, rowf(i), colb + h))

    def cs_spec(rowf):
        return pl.BlockSpec((blk, 2 * HEAD_DIM), lambda b, i, h: (rowf(i), 0))

    return pl.pallas_call(
        functools.partial(_attn_kernel, n_ctx_blocks=ncb, s_len=s_len),
        out_shape=jax.ShapeDtypeStruct((bn, l_len, A_Q), F32),
        grid=(bn, nb, A_KV_HEADS),
        in_specs=[pl.BlockSpec((1, blk, qw), lambda b, i, h: (b, i, q_col // qw + h)),
                  kv_spec(kb, prev), kv_spec(kb, cur), kv_spec(kb, nxt),
                  kv_spec(vb, prev), kv_spec(vb, cur), kv_spec(vb, nxt),
                  pl.BlockSpec((1, n_ctx, HEAD_DIM), lambda b, i, h: (b, 0, kb + h)),
                  pl.BlockSpec((1, n_ctx, HEAD_DIM), lambda b, i, h: (b, 0, vb + h)),
                  cs_spec(prev), pl.BlockSpec((blk, 2 * HEAD_DIM), lambda b, i, h: (i, 0)), cs_spec(nxt),
                  pl.BlockSpec((1, 1, grp), lambda b, i, h: (h, 0, 0))],
        out_specs=pl.BlockSpec((1, blk, qw), lambda b, i, h: (b, i, h)),
        compiler_params=pltpu.CompilerParams(dimension_semantics=("parallel", "parallel", "parallel"),
                                             vmem_limit_bytes=VMEM_LIMIT),
        name="window_attention",
    )(p_all, p_all, p_all, p_all, p_all, p_all, p_all, p_all, p_all, cs, cs, cs,
      sink.astype(F32).reshape(A_KV_HEADS, 1, grp))


def rope_table(n_ctx, s_len):
    rows = s_len // GRID_W
    row = jnp.repeat(jnp.arange(rows, dtype=F32), GRID_W)
    colp = jnp.tile(jnp.arange(GRID_W, dtype=F32), rows)
    n_freq = HEAD_DIM // 4
    inv_freq = ROPE_THETA ** (-jnp.arange(n_freq, dtype=F32) / n_freq)
    ang_r = row[:, None] * inv_freq
    ang_c = colp[:, None] * inv_freq
    cos = jnp.concatenate([jnp.cos(ang_r), jnp.cos(ang_r), jnp.cos(ang_c), jnp.cos(ang_c)], -1)
    sin = jnp.concatenate([-jnp.sin(ang_r), jnp.sin(ang_r), -jnp.sin(ang_c), jnp.sin(ang_c)], -1)
    lat = jnp.concatenate([cos, sin], -1)
    ctx = jnp.concatenate([jnp.ones((n_ctx, HEAD_DIM), F32), jnp.zeros((n_ctx, HEAD_DIM), F32)], -1)
    return jnp.concatenate([ctx, lat], 0)


def _order_masks(c, d):
    t = lax.broadcasted_iota(jnp.int32, (c, c), 0)
    j = lax.broadcasted_iota(jnp.int32, (c, c), 1)
    rel = (t - j) * (1 - 2 * d)
    return rel >= 0, rel > 0


def _unit_inverses(ms):
    c = ms[0].shape[0]
    eye = (lax.broadcasted_iota(jnp.int32, (c, c), 0) == lax.broadcasted_iota(jnp.int32, (c, c), 1)).astype(F32)
    ts = [eye + m for m in ms]
    pws = [_dot(m, m) for m in ms]
    n = 2
    while 2 * n < c:
        st = [_dot(jnp.concatenate([t, p], 0), p) for t, p in zip(ts, pws)]
        ts = [t + s[:c] for t, s in zip(ts, st)]
        pws = [s[c:] for s in st]
        n *= 2
    return [t + _dot(t, p) for t, p in zip(ts, pws)]


def _scan_chunk(hts, r_t, a_t, v, k_h, b_h, pc, n_rk, n_rb=None, m_ab=None, m_ak=None):
    zipm = lambda f, *ls: [f(*xs) for xs in zip(*ls)]
    if m_ab is None:
        y0 = zipm(_dot, n_rk, v)
        s0 = zipm(_dot_tn, v, k_h)
        ys = zipm(lambda r, ht, y: _dot_nt(r, ht) + y, r_t, hts, y0)
        return ys, zipm(lambda ht, p, s: ht * p + s, hts, pc, s0)
    dk = a_t[0].shape[1]
    ts = _unit_inverses(m_ab)
    mv = zipm(_dot, m_ak, v)
    wu = zipm(lambda t, a, m: _dot(t, jnp.concatenate([a, m], 1)), ts, a_t, mv)
    nwu = zipm(_dot, n_rb, wu)
    nkv = zipm(_dot, n_rk, v)
    gs = zipm(_dot_tn, wu, b_h)
    vk = zipm(_dot_tn, v, k_h)
    ys = zipm(lambda r, n, ht, y: _dot_nt(r + n[:, :dk], ht) + n[:, dk:] + y, r_t, nwu, hts, nkv)
    return ys, zipm(lambda ht, p, g_, s: ht * p + _dot(ht, g_[:dk]) + g_[dk:] + s, hts, pc, gs, vk)


def _time_block(d, j, ncc, nct):
    bwd = jnp.where(j < ncc, ncc - 1 - j, nct - 1 - j + ncc)
    return jnp.where(d == 0, j, bwd)


def _cum_scalars(col_g, row_g, incl):
    tri = incl.astype(F32)
    cum_col = _dot(tri, col_g, HIGHEST)
    cum_row = _dot_nt(row_g, tri, HIGHEST)
    tot = jnp.sum(col_g, 0, keepdims=True)
    return cum_col, cum_row, tot


def _dn_chunk_args(q_ref, k_ref, v_ref, col_ref, row_ref, d, heads, dh):
    c = q_ref.shape[1]
    incl, strict = _order_masks(c, d)
    col = col_ref[0, 0]
    row = row_ref[0, 0, 0]
    cum_col, cum_row, tot = _cum_scalars(col[:, heads:], row[heads:], incl)
    hs = range(heads)
    sls = [slice(h * dh, (h + 1) * dh) for h in hs]
    qs, ks, vs = ([ref[0, :, sl] for sl in sls] for ref in (q_ref, k_ref, v_ref))
    kks = [_dot_nt(k, k) for k in ks]
    qks = [_dot_nt(q, k) for q, k in zip(qs, ks)]
    args = [[] for _ in range(10)]
    for h in hs:
        q, k = qs[h], ks[h]
        bc, gc = col[:, h:h + 1], col[:, heads + h:heads + h + 1]
        br, gr = row[h:h + 1], row[heads + h:heads + h + 1]
        cc, cr, tt = cum_col[:, h:h + 1], cum_row[h:h + 1], tot[:, h:h + 1]
        kkb = kks[h] * br
        qkb = qks[h] * br
        dif = cc - cr
        m_ak = kkb * jnp.exp(jnp.where(strict, dif - gc, NEG_BIG))
        m_ab = -kkb * jnp.exp(jnp.where(strict, dif - gc + gr, NEG_BIG))
        n_rk = qkb * jnp.exp(jnp.where(incl, dif, NEG_BIG))
        n_rb = -qkb * jnp.exp(jnp.where(incl, dif + gr, NEG_BIG))
        a_t = k * jnp.exp(cc - gc)
        r_t = q * jnp.exp(cc)
        k_h = k * (bc * jnp.exp(tt - cc))
        b_h = -k * (bc * jnp.exp(gc + tt - cc))
        for lst, val in zip(args, (r_t, a_t, vs[h], k_h, b_h, jnp.exp(tt), n_rk, n_rb, m_ab, m_ak)):
            lst.append(val)
    return args


def _dn_scan_kernel(qf_ref, kf_ref, vf_ref, colf_ref, rowf_ref, qb_ref, kb_ref, vb_ref, colb_ref, rowb_ref,
                    of_ref, ob_ref, st_ref, *, heads, dh):
    @pl.when(pl.program_id(1) == 0)
    def _():
        st_ref[...] = jnp.zeros_like(st_ref)

    fwd = _dn_chunk_args(qf_ref, kf_ref, vf_ref, colf_ref, rowf_ref, 0, heads, dh)
    bwd = _dn_chunk_args(qb_ref, kb_ref, vb_ref, colb_ref, rowb_ref, 1, heads, dh)
    ys, hts = _scan_chunk([st_ref[h] for h in range(2 * heads)], *[f + b for f, b in zip(fwd, bwd)])
    for h in range(2 * heads):
        st_ref[h] = hts[h]
    for h in range(heads):
        of_ref[0, :, h * dh:(h + 1) * dh] = ys[h]
        ob_ref[0, :, h * dh:(h + 1) * dh] = ys[heads + h]


def dn_scan(q, k, v, beta, g, n_ctx, col_blocks=(0, 0, 0)):
    bn, l_len, _ = q.shape
    heads = beta.shape[-1]
    dh = DN_HEAD_DIM
    w = heads * dh
    c = SCAN_CHUNK
    nct, ncc = l_len // c, n_ctx // c
    col = jnp.concatenate([jnp.broadcast_to(beta[None], g.shape), g], -1)
    row = jnp.swapaxes(col.reshape(2, bn, nct, c, 2 * heads), 3, 4)
    tb = lambda d, j: _time_block(d, j, ncc, nct)
    seq = lambda d, cb: pl.BlockSpec((1, c, w), lambda b, j: (b, tb(d, j), cb))
    side = lambda d: [seq(d, col_blocks[0]), seq(d, col_blocks[1]), seq(d, col_blocks[2]),
                      pl.BlockSpec((1, 1, c, 2 * heads), lambda b, j: (d, b, tb(d, j), 0)),
                      pl.BlockSpec((1, 1, 1, 2 * heads, c), lambda b, j: (d, b, tb(d, j), 0, 0))]
    out = jax.ShapeDtypeStruct((bn, l_len, w), F32)
    return pl.pallas_call(
        functools.partial(_dn_scan_kernel, heads=heads, dh=dh),
        out_shape=[out, out],
        grid=(bn, nct),
        in_specs=side(0) + side(1),
        out_specs=[pl.BlockSpec((1, c, w), lambda b, j: (b, tb(0, j), 0)),
                   pl.BlockSpec((1, c, w), lambda b, j: (b, tb(1, j), 0))],
        scratch_shapes=[pltpu.VMEM((2 * heads, dh, dh), F32)],
        compiler_params=pltpu.CompilerParams(dimension_semantics=("parallel", "arbitrary")),
        name="dn_scan",
    )(q, k, v, col, row, q, k, v, col, row)


def _ssd_scan_kernel(c_ref, b_ref, x_ref, col_ref, row_ref, o_ref, st_ref, *, heads, groups, dh, n_st):
    d = pl.program_id(1)

    @pl.when(pl.program_id(2) == 0)
    def _():
        st_ref[...] = jnp.zeros_like(st_ref)

    c = c_ref.shape[1]
    incl, _ = _order_masks(c, d)
    col = col_ref[0, 0]
    row = row_ref[0, 0, 0]
    cum_col, cum_row, tot = _cum_scalars(col[:, heads:], row, incl)
    hg = heads // groups
    gw = hg * dh
    spread = (lax.broadcasted_iota(jnp.int32, (hg, gw), 1) // dh
              == lax.broadcasted_iota(jnp.int32, (hg, gw), 0)).astype(F32)
    for g in range(groups):
        hsl = slice(g * hg, (g + 1) * hg)
        cm = c_ref[0, :, g * n_st:(g + 1) * n_st]
        bm = b_ref[0, :, g * n_st:(g + 1) * n_st]
        cb = _dot_nt(cm, bm)
        cum_g = cum_col[:, hsl]
        dt_x = _dot_sel(col[:, hsl], spread)
        ecc_x = _dot_sel(jnp.exp(cum_g), spread)
        eend_x = _dot_sel(jnp.exp(tot[:, hsl] - cum_g), spread)
        v = x_ref[0, :, g * gw:(g + 1) * gw] * dt_x
        hst = st_ref[g]
        y_off = _dot_nt(cm, hst) * ecc_x
        upd = _dot_tn(v * eend_x, bm)
        for h in range(hg):
            ha = g * hg + h
            sl = slice(h * dh, (h + 1) * dh)
            n_rk = cb * jnp.exp(jnp.where(incl, cum_col[:, ha:ha + 1] - cum_row[ha:ha + 1], NEG_BIG))
            o_ref[0, 0, :, g * gw + h * dh:g * gw + (h + 1) * dh] = y_off[:, sl] + _dot(n_rk, v[:, sl])
            st_ref[g, sl, :] = hst[sl] * jnp.exp(tot[:, ha:ha + 1]) + upd[sl]


def ssd_scan(cm, bm, x, dt, da, n_ctx, col_blocks=(0, 0, 0)):
    bn, l_len, _ = x.shape
    heads = dt.shape[-1]
    dh, n_st, grp = SSD_HEAD_DIM, SSD_STATE, SSD_GROUPS
    cb0, bb0, xb0 = col_blocks
    c = SCAN_CHUNK
    nct, ncc = l_len // c, n_ctx // c
    col = jnp.concatenate([dt, da], -1)
    row = jnp.swapaxes(da.reshape(2, bn, nct, c, heads), 3, 4)
    tb = lambda d, j: _time_block(d, j, ncc, nct)
    gseq = lambda off: pl.BlockSpec((1, c, grp * n_st), lambda b, d, j: (b, tb(d, j), off))
    return pl.pallas_call(
        functools.partial(_ssd_scan_kernel, heads=heads, groups=grp, dh=dh, n_st=n_st),
        out_shape=jax.ShapeDtypeStruct((2, bn, l_len, heads * dh), F32),
        grid=(bn, 2, nct),
        in_specs=[gseq(cb0), gseq(bb0),
                  pl.BlockSpec((1, c, heads * dh), lambda b, d, j: (b, tb(d, j), xb0)),
                  pl.BlockSpec((1, 1, c, 2 * heads), lambda b, d, j: (d, b, tb(d, j), 0)),
                  pl.BlockSpec((1, 1, 1, heads, c), lambda b, d, j: (d, b, tb(d, j), 0, 0))],
        out_specs=pl.BlockSpec((1, 1, c, heads * dh), lambda b, d, j: (d, b, tb(d, j), 0)),
        scratch_shapes=[pltpu.VMEM((grp, heads // grp * dh, n_st), F32)],
        compiler_params=pltpu.CompilerParams(dimension_semantics=("parallel", "parallel", "arbitrary")),
        name="ssd_scan",
    )(cm, bm, x, col, row)


def _rwkv_scan_kernel(r_ref, k_ref, v_ref, kk_ref, b_ref, lw_ref, o_ref, st_ref, *, heads, dh):
    d = pl.program_id(2)

    @pl.when(pl.program_id(3) == 0)
    def _():
        st_ref[...] = jnp.zeros_like(st_ref)

    c = r_ref.shape[1]
    incl, strict = _order_masks(c, d)
    lw_all = lw_ref[0, 0]
    cl_all = _sel_dot(incl, lw_all)
    tot_all = jnp.sum(lw_all, 0, keepdims=True)
    hs = range(heads)
    sls = [slice(h * dh, (h + 1) * dh) for h in hs]
    args = [[] for _ in range(6)]
    grams = []
    for h in hs:
        sl = sls[h]
        r, k, v, kk, b = r_ref[0, :, sl], k_ref[0, :, sl], v_ref[0, :, sl], kk_ref[0, :, sl], b_ref[0, :, sl]
        lw, cl, tt = lw_all[:, sl], cl_all[:, sl], tot_all[:, sl]
        inv_p = jnp.exp(-cl)
        w_end = jnp.exp(tt - cl)
        a_t = -kk * jnp.exp(cl - lw)
        r_t = r * jnp.exp(cl)
        grams.append((a_t, r_t, b * inv_p, k * inv_p))
        for lst, val in zip(args, (r_t, a_t, v, k * w_end, b * w_end, jnp.exp(tt))):
            lst.append(val)
    gm = [_dot_nt(jnp.concatenate([a, r], 0), jnp.concatenate([b, k], 0)) for a, r, b, k in grams]
    m_ab = [jnp.where(strict, t[:c, :c], 0.0) for t in gm]
    m_ak = [jnp.where(strict, t[:c, c:], 0.0) for t in gm]
    n_rb = [jnp.where(incl, t[c:, :c], 0.0) for t in gm]
    n_rk = [jnp.where(incl, t[c:, c:], 0.0) for t in gm]
    ys, hts = _scan_chunk([st_ref[h] for h in hs], *args, n_rk, n_rb, m_ab, m_ak)
    for h in hs:
        st_ref[h] = hts[h]
        o_ref[0, 0, :, sls[h]] = ys[h]


def rwkv_scan(r, k, v, kk, b, lw, n_ctx, heads_per_step=32):
    bn, l_len, w = r.shape
    dh = RW_HEAD_DIM
    hb = heads_per_step
    c = SCAN_CHUNK
    nct, ncc = l_len // c, n_ctx // c
    tb = lambda d, j: _time_block(d, j, ncc, nct)
    seq = pl.BlockSpec((1, c, hb * dh), lambda b_, g, d, j: (b_, tb(d, j), g))
    return pl.pallas_call(
        functools.partial(_rwkv_scan_kernel, heads=hb, dh=dh),
        out_shape=jax.ShapeDtypeStruct((2, bn, l_len, w), F32),
        grid=(bn, w // (hb * dh), 2, nct),
        in_specs=[seq, seq, seq, seq, seq,
                  pl.BlockSpec((1, 1, c, hb * dh), lambda b_, g, d, j: (d, b_, tb(d, j), g))],
        out_specs=pl.BlockSpec((1, 1, c, hb * dh), lambda b_, g, d, j: (d, b_, tb(d, j), g)),
        scratch_shapes=[pltpu.VMEM((hb, dh, dh), F32)],
        compiler_params=pltpu.CompilerParams(
            dimension_semantics=("parallel", "parallel", "parallel", "arbitrary")),
        name="rwkv_scan",
    )(r, k, v, kk, b, lw)


def _router_kernel(h_ref, wr_ref, bias_ref, rank_ref, wt_ref, cnt_ref):
    tm = h_ref.shape[0]
    ne, ng = N_EXPERTS, N_GROUPS
    per = ne // ng
    h = h_ref[...]
    logits = sum(_dot_nt(piece, h) for piece in _split3(wr_ref[...]))
    scores = jax.nn.sigmoid(logits)
    sel = scores + bias_ref[...]
    sel3 = sel.reshape(ng, per, tm)
    idx3 = lax.broadcasted_iota(jnp.int32, (ng, per, tm), 1)
    m1 = jnp.max(sel3, 1, keepdims=True)
    first = jnp.min(jnp.where(sel3 == m1, idx3, per), 1, keepdims=True)
    m2 = jnp.max(jnp.where(idx3 == first, -jnp.inf, sel3), 1, keepdims=True)
    grp = (m1 + m2).reshape(ng, tm)
    gidx = lax.broadcasted_iota(jnp.int32, (ng, tm), 0)
    chosen = jnp.zeros((ng, tm), jnp.bool_)
    for _ in range(TOPK_GROUPS):
        gm = jnp.max(grp, 0, keepdims=True)
        gi = jnp.min(jnp.where(grp == gm, gidx, ng), 0, keepdims=True)
        hit = gidx == gi
        chosen = chosen | hit
        grp = jnp.where(hit, -jnp.inf, grp)
    selm = jnp.where(chosen.reshape(ng, 1, tm), sel3, -jnp.inf).reshape(ne, tm)
    eidx = lax.broadcasted_iota(jnp.int32, (ne, tm), 0)
    picked = jnp.zeros((ne, tm), jnp.bool_)
    for _ in range(TOP_K):
        em = jnp.max(selm, 0, keepdims=True)
        ei = jnp.min(jnp.where(selm == em, eidx, ne), 0, keepdims=True)
        hit = eidx == ei
        picked = picked | hit
        selm = jnp.where(hit, -jnp.inf, selm)
    wsel = jnp.where(picked, scores, 0.0)
    wt_ref[...] = wsel / jnp.sum(wsel, 0, keepdims=True) * ROUTED_SCALE
    onehot = picked.astype(BF16)
    before = (lax.broadcasted_iota(jnp.int32, (tm, tm), 0) < lax.broadcasted_iota(jnp.int32, (tm, tm), 1))
    rank = _dot(onehot, before.astype(BF16))
    rank_ref[...] = jnp.where(picked, rank, -1.0)
    cnt = jnp.sum(picked.astype(F32), 1, keepdims=True)
    cnt_ref[0] = jnp.broadcast_to(cnt, (ne, LANES)).astype(jnp.int32)


def _moe_block_rows(sub):
    mean = sub * TOP_K / N_EXPERTS
    return int(-(-(mean + 2.25 * mean ** 0.5) // 16) * 16)


def _expert_kernel(cnt_ref, h_ref, rank_ref, wt_ref, wg_ref, wu_ref, wd_ref, sg_ref, su_ref, sd_ref, o_ref,
                   xg_sc, gw_sc, y_sc, *, sub, group):
    i, e = pl.program_id(0), pl.program_id(1)
    tm = h_ref.shape[0]
    n_sub = tm // sub
    rows = _moe_block_rows(sub)
    d = o_ref.shape[1]
    cw = 512

    @pl.when(e == 0)
    def _():
        for s in range(n_sub):
            hs = h_ref[s * sub:(s + 1) * sub, :]
            act = (jax.nn.silu(_dot(hs, sg_ref[...])) * _dot(hs, su_ref[...])).astype(BF16)
            o_ref[s * sub:(s + 1) * sub, :] = _dot(act, sd_ref[...])

    ge = e % group
    r_iota = lax.broadcasted_iota(jnp.int32, (rows, sub), 0).astype(F32)

    def expert(xg):
        act = jax.nn.silu(_dot(xg, wg_ref[0])) * _dot(xg, wu_ref[0])
        return _dot(act.astype(BF16), wd_ref[0]).astype(BF16)

    @pl.when(ge == 0)
    def _():
        for s in range(n_sub):
            cols = slice(s * sub, (s + 1) * sub)
            hits = [rank_ref[q][:, cols] == r_iota for q in range(group)]
            got = _dot(jnp.concatenate(hits, 0).astype(BF16), h_ref[cols, :]).astype(BF16)
            for q in range(group):
                xg_sc[q, s * rows:(s + 1) * rows, :] = got[q * rows:(q + 1) * rows]
            gw_sc[s] = jnp.concatenate([jnp.where(hits[q], wt_ref[q][:, cols], 0.0) for q in range(group)],
                                       0).astype(BF16)

    y = expert(xg_sc[ge])
    off = pl.multiple_of(ge * rows, rows)
    for s in range(n_sub):
        y_sc[s, pl.ds(off, rows), :] = y[s * rows:(s + 1) * rows]

    @pl.when(ge == group - 1)
    def _():
        for s in range(n_sub):
            for c0 in range(0, d, cw):
                o_ref[s * sub:(s + 1) * sub, c0:c0 + cw] += _dot_tn(gw_sc[s], y_sc[s, :, c0:c0 + cw])

    rk = rank_ref[ge]
    wt = wt_ref[ge]
    c_max = cnt_ref[i * n_sub * N_EXPERTS + e]
    for s in range(1, n_sub):
        c_max = jnp.maximum(c_max, cnt_ref[(i * n_sub + s) * N_EXPERTS + e])
    n_blk = (c_max + rows - 1) // rows

    def body(b, carry):
        base = r_iota + (b * rows).astype(F32)
        hits = [rk[:, s * sub:(s + 1) * sub] == base for s in range(n_sub)]
        xg = jnp.concatenate([_dot(hit.astype(BF16), h_ref[s * sub:(s + 1) * sub, :])
                              for s, hit in enumerate(hits)], 0).astype(BF16)
        yb = expert(xg)
        for s, hit in enumerate(hits):
            gw = jnp.where(hit, wt[:, s * sub:(s + 1) * sub], 0.0).astype(BF16)
            for c0 in range(0, d, cw):
                o_ref[s * sub:(s + 1) * sub, c0:c0 + cw] += _dot_tn(gw, yb[s * rows:(s + 1) * rows, c0:c0 + cw])
        return carry

    lax.fori_loop(1, n_blk, body, 0)


def moe(h, w_router, router_bias, we_gate, we_up, we_down, ws_gate, ws_up, ws_down, tm=1536, sub=384, group=4):
    n_tok, d = h.shape
    ne = N_EXPERTS
    nt = n_tok // tm
    ff = we_gate.shape[-1]
    rank, wt, cnt = pl.pallas_call(
        _router_kernel,
        out_shape=[jax.ShapeDtypeStruct((ne, n_tok), F32), jax.ShapeDtypeStruct((ne, n_tok), F32),
                   jax.ShapeDtypeStruct((n_tok // sub, ne, LANES), jnp.int32)],
        grid=(n_tok // sub,),
        in_specs=[pl.BlockSpec((sub, d), lambda i: (i, 0)), pl.BlockSpec((ne, d), lambda i: (0, 0)),
                  pl.BlockSpec((ne, 1), lambda i: (0, 0))],
        out_specs=[pl.BlockSpec((ne, sub), lambda i: (0, i)), pl.BlockSpec((ne, sub), lambda i: (0, i)),
                   pl.BlockSpec((1, ne, LANES), lambda i: (i, 0, 0))],
        compiler_params=pltpu.CompilerParams(dimension_semantics=("parallel",), vmem_limit_bytes=VMEM_LIMIT),
        name="moe_router",
    )(h, w_router.T.astype(F32), router_bias.astype(F32).reshape(ne, 1))
    counts = cnt[:, :, 0].reshape(-1)
    const = lambda i, e, c: (0, 0)
    once = pl.Buffered(1)
    rows = _moe_block_rows(sub)
    n_sub = tm // sub
    return pl.pallas_call(
        functools.partial(_expert_kernel, sub=sub, group=group),
        out_shape=jax.ShapeDtypeStruct((n_tok, d), F32),
        grid_spec=pltpu.PrefetchScalarGridSpec(
            num_scalar_prefetch=1,
            grid=(nt, ne),
            scratch_shapes=[pltpu.VMEM((group, n_sub * rows, d), BF16),
                            pltpu.VMEM((n_sub, group * rows, sub), BF16),
                            pltpu.VMEM((n_sub, group * rows, d), BF16)],
            in_specs=[pl.BlockSpec((tm, d), lambda i, e, c: (i, 0), pipeline_mode=once),
                      pl.BlockSpec((group, 1, tm), lambda i, e, c: (e // group, 0, i)),
                      pl.BlockSpec((group, 1, tm), lambda i, e, c: (e // group, 0, i)),
                      pl.BlockSpec((1, d, ff), lambda i, e, c: (e, 0, 0)),
                      pl.BlockSpec((1, d, ff), lambda i, e, c: (e, 0, 0)),
                      pl.BlockSpec((1, ff, d), lambda i, e, c: (e, 0, 0)),
                      pl.BlockSpec(ws_gate.shape, const, pipeline_mode=once),
                      pl.BlockSpec(ws_up.shape, const, pipeline_mode=once),
                      pl.BlockSpec(ws_down.shape, const, pipeline_mode=once)],
            out_specs=pl.BlockSpec((tm, d), lambda i, e, c: (i, 0), pipeline_mode=once)),
        compiler_params=pltpu.CompilerParams(dimension_semantics=("parallel", "arbitrary"),
                                             vmem_limit_bytes=VMEM_LIMIT),
        name="moe_experts",
    )(counts, h, rank.reshape(ne, 1, n_tok), wt.reshape(ne, 1, n_tok),
      we_gate.astype(BF16), we_up.astype(BF16), we_down.astype(BF16),
      ws_gate.astype(BF16), ws_up.astype(BF16), ws_down.astype(BF16))


def _segments(t, n_ctx, fn):
    return jnp.concatenate([fn(t[:, :n_ctx]), fn(t[:, n_ctx:])], 1)


def _dw_conv(t, w):
    pad = w.shape[0] // 2
    return lax.conv_general_dilated(
        t, w[:, None, :].astype(t.dtype), window_strides=(1,), padding=((pad, pad),),
        dimension_numbers=('NWC', 'WIO', 'NWC'), feature_group_count=t.shape[-1])


def _l2n(t):
    return t * lax.rsqrt(jnp.sum(jnp.square(t), -1, keepdims=True) + RMS_EPS)


def _heads(t, n):
    return t.reshape(*t.shape[:-1], n, t.shape[-1] // n)


def _pad_cols(w, width):
    return jnp.pad(w, ((0, 0), (0, width - w.shape[1])))


def _pad_rows(w, height):
    return jnp.pad(w, ((0, height - w.shape[0]), (0, 0)))


def _mod_vectors(c, c_ctx, w_mod, b_mod):
    bn, d = c.shape
    cc = jnp.concatenate([c, c_ctx[None], jnp.zeros((8 - bn - 1, d), F32)], 0)
    mv = matmul([(jax.nn.silu(cc), w_mod.astype(BF16))]) + b_mod
    mv = mv.reshape(8, 6, d)
    lat = mv[:bn]
    ctx = jnp.broadcast_to(mv[bn][None], lat.shape)
    return jnp.stack([ctx, lat], 1)


def _pack_mv(gate, shift, scale):
    z = jnp.zeros_like(gate)
    return jnp.stack([gate, shift, scale, z, z, z, z, z], 2)


def _cat_pad(parts):
    return jnp.concatenate([_pad_cols(a, wd) if a.ndim == 2 else jnp.pad(a, (0, wd - a.shape[0]))
                            for a, wd in parts], -1)


def _seg_allsum(x, seg):
    if seg == LANES:
        parts = [x[:, c0:c0 + LANES] for c0 in range(0, x.shape[-1], LANES)]
        parts = [jnp.broadcast_to(jnp.sum(t, -1, keepdims=True), t.shape) for t in parts]
        return parts[0] if len(parts) == 1 else jnp.concatenate(parts, -1)
    bw = 2 * LANES
    same = (lax.broadcasted_iota(jnp.int32, (bw, bw), 0) // seg) == (lax.broadcasted_iota(jnp.int32, (bw, bw), 1) // seg)
    ones_bd = jnp.where(same, 1.0, 0.0).astype(BF16)
    hi, mid, lo = _split3(x)
    parts = []
    for c0 in range(0, x.shape[-1], bw):
        sl = slice(c0, c0 + bw)
        parts.append(_dot(hi[:, sl], ones_bd) + _dot(mid[:, sl], ones_bd) + _dot(lo[:, sl], ones_bd))
    return parts[0] if len(parts) == 1 else jnp.concatenate(parts, -1)


def _halo_rows(x, xp_ref, xn_ref, i, n_ctx_tiles):
    has_prev = (i > 0) & (i != n_ctx_tiles)
    has_next = (i != n_ctx_tiles - 1) & (i != pl.num_programs(1) - 1)
    xp = jnp.where(has_prev, xp_ref[0], 0.0)
    xn = jnp.where(has_next, xn_ref[0], 0.0)
    return jnp.concatenate([xp, x, xn], 0)


def _halo_specs(tr, width, l_len, col_block):
    nh, n8 = tr // 8, l_len // 8
    return [pl.BlockSpec((1, tr, width), lambda b, i, *s: (b, i, col_block(*s))),
            pl.BlockSpec((1, 8, width), lambda b, i, *s: (b, jnp.maximum(i * nh - 1, 0), col_block(*s))),
            pl.BlockSpec((1, 8, width), lambda b, i, *s: (b, jnp.minimum((i + 1) * nh, n8 - 1), col_block(*s)))]


def _conv_kernel(x_ref, xp_ref, xn_ref, w_ref, b_ref, o_ref, *, n_ctx_tiles, kw, l2_segments):
    i, s = pl.program_id(1), pl.program_id(2)
    x = x_ref[0]
    tr = x.shape[0]
    xe = _halo_rows(x, xp_ref, xn_ref, i, n_ctx_tiles)
    pad = kw // 2
    acc = jnp.broadcast_to(b_ref[...], x.shape)
    for j in range(kw):
        acc = acc + w_ref[j:j + 1, :] * xe[8 - pad + j:8 - pad + j + tr]
    y = acc * jax.nn.sigmoid(acc)
    if l2_segments:
        yn = y * lax.rsqrt(_seg_allsum(y * y, DN_HEAD_DIM) + RMS_EPS)
        yn = yn * jnp.where(s == 0, DN_HEAD_DIM ** -0.5, 1.0)
        y = jnp.where(s < 2, yn, y)
    o_ref[0] = y


def conv_silu(p, col0, width, seg_w, conv_w, conv_b, n_ctx, l2_segments=False):
    bn, l_len, _ = p.shape
    tr = 256
    cb = col0 // seg_w
    return pl.pallas_call(
        functools.partial(_conv_kernel, n_ctx_tiles=n_ctx // tr, kw=conv_w.shape[0], l2_segments=l2_segments),
        out_shape=jax.ShapeDtypeStruct((bn, l_len, width), F32),
        grid=(bn, l_len // tr, width // seg_w),
        in_specs=_halo_specs(tr, seg_w, l_len, lambda s: cb + s)
        + [pl.BlockSpec((8, seg_w), lambda b, i, s: (0, s)), pl.BlockSpec((1, seg_w), lambda b, i, s: (0, s))],
        out_specs=pl.BlockSpec((1, tr, seg_w), lambda b, i, s: (b, i, s)),
        compiler_params=pltpu.CompilerParams(dimension_semantics=("parallel", "parallel", "parallel"),
                                             vmem_limit_bytes=VMEM_LIMIT),
        name="conv_silu",
    )(p, p, p, _pad_rows(conv_w.astype(F32), 8), conv_b.astype(F32).reshape(1, width))


def _softplus(x):
    return jnp.maximum(x, 0.0) + jnp.log1p(jnp.exp(-jnp.abs(x)))


def _rwkv_prep_kernel(x_ref, xp_ref, xn_ref, mu_ref, vec_ref, a2_ref, g2_ref, w2f_ref, w2b_ref,
                      r_ref, k_ref, v_ref, kk_ref, b_ref, lw_ref, g_ref, *, n_ctx_tiles):
    x = x_ref[0]
    tr = x.shape[0]
    xe = _halo_rows(x, xp_ref, xn_ref, pl.program_id(1), n_ctx_tiles)
    x = x + mu_ref[...] * (0.5 * (xe[7:7 + tr] + xe[9:9 + tr]) - x)
    w = RW_W
    r, k, v = x[:, :w], x[:, w:2 * w], x[:, 2 * w:3 * w]
    o = 3 * w
    wl_f, wl_b, al = x[:, o:o + LANES], x[:, o + LANES:o + 2 * LANES], x[:, o + 2 * LANES:o + 3 * LANES]
    gl = x[:, o + 3 * LANES:o + 3 * LANES + RW_GATE_LORA]
    vec = vec_ref[...]
    a = jax.nn.sigmoid(vec[0:1] + _dot(al, a2_ref[...]))
    g_ref[0] = _dot(jax.nn.sigmoid(gl), g2_ref[...])
    kk = k * vec[1:2]
    kk = kk * lax.rsqrt(_seg_allsum(kk * kk, RW_HEAD_DIM) + RMS_EPS)
    r_ref[0] = r
    v_ref[0] = v
    k_ref[0] = k * (1.0 + (a - 1.0) * vec[2:3])
    kk_ref[0] = kk
    b_ref[0] = kk * a
    for di, (wl, w2_ref) in enumerate(((wl_f, w2f_ref), (wl_b, w2b_ref))):
        w_log = -_softplus(-(vec[3 + di:4 + di] + _dot(jnp.tanh(wl), w2_ref[...]))) - 0.5
        lw_ref[di, 0] = -jnp.exp(w_log)


def rwkv_prep(p_rw, mu, vec, a2, g2, w2f, w2b, n_ctx):
    bn, l_len, wp = p_rw.shape
    tr = 128
    w = RW_W
    full = lambda a: pl.BlockSpec(a.shape, lambda b, i: (0,) * a.ndim)
    seq = pl.BlockSpec((1, tr, w), lambda b, i: (b, i, 0))
    sds = jax.ShapeDtypeStruct((bn, l_len, w), F32)
    return pl.pallas_call(
        functools.partial(_rwkv_prep_kernel, n_ctx_tiles=n_ctx // tr),
        out_shape=[sds, sds, sds, sds, sds, jax.ShapeDtypeStruct((2, bn, l_len, w), F32), sds],
        grid=(bn, l_len // tr),
        in_specs=_halo_specs(tr, wp, l_len, lambda: 0) + [full(mu), full(vec), full(a2), full(g2), full(w2f), full(w2b)],
        out_specs=[seq, seq, seq, seq, seq, pl.BlockSpec((2, 1, tr, w), lambda b, i: (0, b, i, 0)), seq],
        compiler_params=pltpu.CompilerParams(dimension_semantics=("parallel", "parallel"),
                                             vmem_limit_bytes=VMEM_LIMIT),
        name="rwkv_prep",
    )(p_rw, p_rw, p_rw, mu, vec, a2, g2, w2f, w2b)


def _l0_out_kernel(att_ref, of_ref, ob_ref, zg_ref, nw_ref, w_ref, z_ref, mv_ref, g_ref, b_ref, zn_ref, h_ref):
    o = of_ref[0] + ob_ref[0]
    o = o * lax.rsqrt(_seg_allsum(o * o, DN_HEAD_DIM) * (1.0 / DN_HEAD_DIM) + RMS_EPS) * nw_ref[...]
    zg = zg_ref[0]
    dn = o * (zg * jax.nn.sigmoid(zg))
    mix = _dot(att_ref[0], w_ref[:A_Q]) + _dot(dn, w_ref[A_Q:])
    _residual_ln_mod(z_ref[0], mix, mv_ref[0, 0], g_ref[...], b_ref[...], zn_ref, h_ref)


def _l1_out_kernel(yf_ref, yb_ref, xs_ref, zs_ref, rf_ref, rb_ref, r_ref, k_ref, v_ref, gt_ref, vec_ref, w_ref,
                   z_ref, mv_ref, g_ref, b_ref, zn_ref, h_ref):
    vec = vec_ref[...]
    zs = zs_ref[0]
    y = (yf_ref[0, 0] + yb_ref[0, 0] + vec[0:1] * xs_ref[0]) * (zs * jax.nn.sigmoid(zs))
    gw = SSD_W // SSD_GROUPS
    parts = []
    for gi in range(SSD_GROUPS):
        yg = y[:, gi * gw:(gi + 1) * gw]
        parts.append(yg * lax.rsqrt(jnp.mean(yg * yg, -1, keepdims=True) + RMS_EPS))
    ssd = jnp.concatenate(parts, -1) * vec[1:2]
    yr = rf_ref[0, 0] + rb_ref[0, 0]
    inv = 1.0 / RW_HEAD_DIM
    yc = yr - _seg_allsum(yr, RW_HEAD_DIM) * inv
    var = _seg_allsum(yc * yc, RW_HEAD_DIM) * inv
    yr = yc * lax.rsqrt(var + RW_LN_EPS) * vec[2:3] + vec[3:4]
    yr = yr + _seg_allsum(r_ref[0] * k_ref[0] * vec[4:5], RW_HEAD_DIM) * v_ref[0]
    rw = yr * gt_ref[0]
    mix = _dot(ssd, w_ref[:SSD_W]) + _dot(rw, w_ref[SSD_W:])
    _residual_ln_mod(z_ref[0], mix, mv_ref[0, 0], g_ref[...], b_ref[...], zn_ref, h_ref)


def _mixer_out_call(body, name, tr, seq_inputs, vec_inputs, w_out, z, mv, ln_g, ln_b, n_ctx):
    bn, l_len, d = z.shape
    nbc = n_ctx // tr
    full = lambda a: pl.BlockSpec(a.shape, lambda b, i: (0,) * a.ndim)
    row = pl.BlockSpec((1, tr, d), lambda b, i: (b, i, 0))
    w_b = w_out.astype(BF16)
    return pl.pallas_call(
        body,
        out_shape=[jax.ShapeDtypeStruct((bn, l_len, d), F32), jax.ShapeDtypeStruct((bn, l_len, d), BF16)],
        grid=(bn, l_len // tr),
        in_specs=[pl.BlockSpec(bs, im) for _, bs, im in seq_inputs] + [full(a) for a in vec_inputs]
        + [pl.BlockSpec(w_b.shape, lambda b, i: (0, 0), pipeline_mode=pl.Buffered(1)), row,
           pl.BlockSpec((1, 1, 8, d), lambda b, i: (b, jnp.where(i < nbc, 0, 1), 0, 0)),
           pl.BlockSpec((1, d), lambda b, i: (0, 0)), pl.BlockSpec((1, d), lambda b, i: (0, 0))],
        out_specs=[row, row],
        compiler_params=pltpu.CompilerParams(dimension_semantics=("parallel", "parallel"),
                                             vmem_limit_bytes=VMEM_LIMIT),
        name=name,
    )(*[a for a, _, _ in seq_inputs], *vec_inputs, w_b, z, mv, ln_g.reshape(1, d), ln_b.reshape(1, d))


def _layer0(z, h, mv, n_ctx, w_in, w_out, attn_sink, conv_w, a_log, dt_bias, norm_w, ln_g, ln_b, cs):
    bn, l_len, d = h.shape
    q, k, v, dq, dk, dv, zz, sm = jnp.split(
        w_in, np.cumsum([A_Q, A_KV, A_KV, DN_W, DN_W, DN_W, DN_W]).tolist(), 1)
    w_in_p = jnp.concatenate([q, dq, dk, dv, zz, k, v, _pad_cols(sm, LANES)], 1).astype(BF16)
    k_col = A_Q + 4 * DN_W
    p = matmul([(h.reshape(bn * l_len, d), w_in_p)]).reshape(bn, l_len, -1)
    att = attention(p, cs, attn_sink, n_ctx, 0, k_col, k_col + A_KV)
    qkv = conv_silu(p, A_Q, 3 * DN_W, DN_W, conv_w, jnp.zeros((3 * DN_W,), F32), n_ctx, l2_segments=True)
    sm_col = k_col + 2 * A_KV
    sm = p[..., sm_col:sm_col + 3 * DN_HEADS]
    a_f, a_b, bb = sm[..., :DN_HEADS], sm[..., DN_HEADS:2 * DN_HEADS], sm[..., 2 * DN_HEADS:]
    beta = jax.nn.sigmoid(bb)
    g = jnp.stack([-jnp.exp(a_log[0]) * jax.nn.softplus(a_f + dt_bias[0]),
                   -jnp.exp(a_log[1]) * jax.nn.softplus(a_b + dt_bias[1])], 0)
    o = dn_scan(qkv, qkv, qkv, beta, g, n_ctx, col_blocks=(0, 1, 2))
    tr = 256
    seq = [(att, (1, tr, A_Q), lambda b, i: (b, i, 0)),
           (o[0], (1, tr, DN_W), lambda b, i: (b, i, 0)),
           (o[1], (1, tr, DN_W), lambda b, i: (b, i, 0)),
           (p, (1, tr, DN_W), lambda b, i: (b, i, (A_Q + 3 * DN_W) // DN_W))]
    return _mixer_out_call(_l0_out_kernel, "l0_out", tr, seq, [jnp.tile(norm_w, DN_HEADS).reshape(1, DN_W)],
                           w_out, z, mv, ln_g, ln_b, n_ctx)


def _layer1(z, h, mv, n_ctx, w_in, w_out, ssd_conv_w, ssd_conv_b, ssd_a_log, ssd_dt_bias, ssd_d, ssd_norm_w,
            rw_mu, rw_w0, rw_w2, rw_a0, rw_a2, rw_g2, rw_k_k, rw_k_a, rw_r_k, rw_ln_g, rw_ln_b, ln_g, ln_b):
    bn, l_len, d = h.shape
    n_tok = bn * l_len
    ssd_in = 2 * SSD_W + 2 * SSD_GN + 2 * SSD_HEADS
    lora_w = [RW_DECAY_LORA, RW_DECAY_LORA, RW_A_LORA]
    h2 = h.reshape(n_tok, d)
    w_ssd = _pad_cols(w_in[:, :ssd_in], 5120).astype(BF16)
    p_ssd = matmul([(h2, w_ssd)]).reshape(bn, l_len, -1)
    xbc = conv_silu(p_ssd, SSD_W, SSD_W + 2 * SSD_GN, 512, ssd_conv_w, ssd_conv_b, n_ctx)
    dtc = p_ssd[..., 2 * SSD_W + 2 * SSD_GN:ssd_in]
    dt = jnp.stack([jax.nn.softplus(dtc[..., :SSD_HEADS] + ssd_dt_bias[0]),
                    jax.nn.softplus(dtc[..., SSD_HEADS:] + ssd_dt_bias[1])], 0)
    da = -jnp.exp(ssd_a_log)[:, None, None, :] * dt
    y_ssd = ssd_scan(xbc, xbc, xbc, dt, da, n_ctx,
                     col_blocks=((SSD_W + SSD_GN) // SSD_GN, SSD_W // SSD_GN, 0))
    splits = np.cumsum([RW_W, RW_W, RW_W] + lora_w).tolist()
    widths = [RW_W, RW_W, RW_W, LANES, LANES, LANES, RW_GATE_LORA]
    w_rw = _cat_pad(list(zip(jnp.split(w_in[:, ssd_in:], splits, 1), widths)))
    w_rw = _pad_cols(w_rw, 6912).astype(BF16)
    mu = _cat_pad(list(zip(jnp.split(rw_mu, splits), widths)))
    mu = jnp.pad(mu, (0, w_rw.shape[1] - mu.shape[0])).reshape(1, -1)
    p_rw = matmul([(h2, w_rw)]).reshape(bn, l_len, -1)
    zero = jnp.zeros((RW_W,), F32)
    vec = jnp.stack([rw_a0, rw_k_k, rw_k_a, rw_w0[0], rw_w0[1], zero, zero, zero], 0)
    lora_p = lambda w: _pad_rows(w, LANES).astype(BF16)
    r, k, v, kk, b, lw, g = rwkv_prep(p_rw, mu, vec, lora_p(rw_a2), rw_g2.astype(BF16), lora_p(rw_w2[0]),
                                      lora_p(rw_w2[1]), n_ctx)
    y_rw = rwkv_scan(r, k, v, kk, b, lw, n_ctx)
    tr = 128
    rowmap = lambda b_, i: (b_, i, 0)
    seq = [(y_ssd, (1, 1, tr, SSD_W), lambda b_, i: (0, b_, i, 0)),
           (y_ssd, (1, 1, tr, SSD_W), lambda b_, i: (1, b_, i, 0)),
           (xbc, (1, tr, SSD_W), rowmap), (p_ssd, (1, tr, SSD_W), rowmap),
           (y_rw, (1, 1, tr, RW_W), lambda b_, i: (0, b_, i, 0)),
           (y_rw, (1, 1, tr, RW_W), lambda b_, i: (1, b_, i, 0)),
           (r, (1, tr, RW_W), rowmap), (k, (1, tr, RW_W), rowmap), (v, (1, tr, RW_W), rowmap),
           (g, (1, tr, RW_W), rowmap)]
    vec_out = jnp.stack([jnp.repeat(ssd_d, SSD_HEAD_DIM), ssd_norm_w, rw_ln_g, rw_ln_b, rw_r_k.reshape(-1),
                         zero, zero, zero], 0)
    return _mixer_out_call(_l1_out_kernel, "l1_out", tr, seq, [vec_out], w_out, z, mv, ln_g, ln_b, n_ctx)


def kernel(x, c, ctx, c_ctx, l0_w_mod, l0_b_mod, l0_ln1_g, l0_ln1_b, l0_ln2_g, l0_ln2_b, l0_w_in, l0_w_out, l0_attn_sink, l0_dn_conv_w, l0_dn_a_log, l0_dn_dt_bias, l0_dn_norm_w, l0_w_router, l0_router_bias, l0_we_gate, l0_we_up, l0_we_down, l0_ws_gate, l0_ws_up, l0_ws_down, l1_w_mod, l1_b_mod, l1_ln1_g, l1_ln1_b, l1_ln2_g, l1_ln2_b, l1_w_in, l1_w_out, l1_ssd_conv_w, l1_ssd_conv_b, l1_ssd_a_log, l1_ssd_dt_bias, l1_ssd_d, l1_ssd_norm_w, l1_rw_mu, l1_rw_w0, l1_rw_w2, l1_rw_a0, l1_rw_a2, l1_rw_g2, l1_rw_k_k, l1_rw_k_a, l1_rw_r_k, l1_rw_ln_g, l1_rw_ln_b, l1_w_router, l1_router_bias, l1_we_gate, l1_we_up, l1_we_down, l1_ws_gate, l1_ws_up, l1_ws_down):
    bn, s_len, d = x.shape
    n_ctx = ctx.shape[1]
    l_len = n_ctx + s_len
    n_tok = bn * l_len
    z = jnp.concatenate([ctx, x], 1)
    cs = rope_table(n_ctx, s_len)
    mv0 = _mod_vectors(c, c_ctx, l0_w_mod, l0_b_mod)
    mv1 = _mod_vectors(c, c_ctx, l1_w_mod, l1_b_mod)
    pick = lambda mv, gi, shi, sci: _pack_mv(mv[:, :, gi], mv[:, :, shi], mv[:, :, sci])

    (h,) = ln_mod(z, z, pick(mv0, 2, 0, 1), l0_ln1_g, l0_ln1_b, n_ctx, do_ln=False, do_mod=True)
    z, h = _layer0(z, h, pick(mv0, 2, 3, 4), n_ctx, l0_w_in, l0_w_out, l0_attn_sink, l0_dn_conv_w, l0_dn_a_log,
                   l0_dn_dt_bias, l0_dn_norm_w, l0_ln1_g, l0_ln1_b, cs)
    f = moe(h.reshape(n_tok, d), l0_w_router, l0_router_bias, l0_we_gate, l0_we_up, l0_we_down,
            l0_ws_gate, l0_ws_up, l0_ws_down).reshape(bn, l_len, d)
    mv_a = _pack_mv(mv0[:, :, 5], mv1[:, :, 0], mv1[:, :, 1])
    z, h = ln_mod(z, f, mv_a, l0_ln2_g, l0_ln2_b, n_ctx, do_ln=True, do_mod=True)

    z, h = _layer1(z, h, pick(mv1, 2, 3, 4), n_ctx, l1_w_in, l1_w_out, l1_ssd_conv_w, l1_ssd_conv_b, l1_ssd_a_log,
                   l1_ssd_dt_bias, l1_ssd_d, l1_ssd_norm_w, l1_rw_mu, l1_rw_w0, l1_rw_w2, l1_rw_a0, l1_rw_a2,
                   l1_rw_g2, l1_rw_k_k, l1_rw_k_a, l1_rw_r_k, l1_rw_ln_g, l1_rw_ln_b, l1_ln1_g, l1_ln1_b)
    f = moe(h.reshape(n_tok, d), l1_w_router, l1_router_bias, l1_we_gate, l1_we_up, l1_we_down,
            l1_ws_gate, l1_ws_up, l1_ws_down).reshape(bn, l_len, d)
    (zx,) = ln_mod(z, f, pick(mv1, 5, 3, 4), l1_ln2_g, l1_ln2_b, n_ctx, do_ln=True, do_mod=False, lat_only=True)
    return zx
```
